```python
import math
import jax, jax.numpy as jnp
from jax import lax
import numpy as np

D_MODEL = 2048
BATCH = 1
SEQ = 16384
DEPTH = 4

N_MIXERS = 2
N_META = 16
BLOCK = 128
WINDOW = 128
A_HEAD_DIM = 64
A_Q_HEADS = D_MODEL // A_HEAD_DIM
A_KV_HEADS = A_Q_HEADS // 8
A_GROUP = A_Q_HEADS // A_KV_HEADS
A_WIDTH = A_Q_HEADS * A_HEAD_DIM
A_KV_WIDTH = A_KV_HEADS * A_HEAD_DIM
A_IN = 2 * A_WIDTH + 2 * A_KV_WIDTH
B_HEAD_DIM = 128
B_HEADS = D_MODEL // B_HEAD_DIM
B_WIDTH = B_HEADS * B_HEAD_DIM
B_IN = 4 * B_WIDTH
B_PAD = BLOCK - N_META
N_BUCKETS = 32
MAX_DISTANCE = 128
N_A_LAYERS = (DEPTH + 1) // 2
N_B_LAYERS = DEPTH // 2
DEEPNORM_ALPHA = (2.0 * DEPTH) ** 0.25
DEEPNORM_BETA = (8.0 * DEPTH) ** -0.25
LN_EPS = 1e-5
NEG = -1e30

kernel_name = "hybrid_swa_sink_stickbreak_deepnorm"


def layer_norm(x, g, b):
    xf = x.astype(jnp.float32)
    mu = jnp.mean(xf, axis=-1, keepdims=True)
    var = jnp.mean(jnp.square(xf - mu), axis=-1, keepdims=True)
    y = (xf - mu) * lax.rsqrt(var + LN_EPS) * g.astype(jnp.float32) + b.astype(jnp.float32)
    return y.astype(x.dtype)


def t5_bucket(dist):
    max_exact = N_BUCKETS // 2
    d = jnp.maximum(dist, 0)
    df = jnp.maximum(d, 1).astype(jnp.float32)
    large = max_exact + (jnp.log(df / max_exact) / math.log(MAX_DISTANCE / max_exact)
                         * (N_BUCKETS - max_exact)).astype(jnp.int32)
    large = jnp.minimum(large, N_BUCKETS - 1)
    return jnp.where(d < max_exact, d, large)


def head_bias(rel_bias, dist):
    return rel_bias[t5_bucket(dist)].astype(jnp.float32)


def sliding_window_sink_attention(q, k, v, sinks, rel_bias):
    f32 = jnp.float32
    b, L = q.shape[0], q.shape[1]
    n_blk = (L - N_META) // BLOCK
    q = q.reshape(b, L, A_KV_HEADS, A_GROUP, A_HEAD_DIM).astype(f32) * (A_HEAD_DIM ** -0.5)
    k = k.reshape(b, L, A_KV_HEADS, A_HEAD_DIM).astype(f32)
    v = v.reshape(b, L, A_KV_HEADS, A_HEAD_DIM).astype(f32)
    qm, km, vm = q[:, :N_META], k[:, :N_META], v[:, :N_META]
    qb = q[:, N_META:].reshape(b, n_blk, BLOCK, A_KV_HEADS, A_GROUP, A_HEAD_DIM)
    kb = k[:, N_META:].reshape(b, n_blk, BLOCK, A_KV_HEADS, A_HEAD_DIM)
    vb = v[:, N_META:].reshape(b, n_blk, BLOCK, A_KV_HEADS, A_HEAD_DIM)
    pad = ((0, 0), (1, 0), (0, 0), (0, 0), (0, 0))
    kl = jnp.concatenate([jnp.pad(kb, pad)[:, :-1], kb], axis=2)
    vl = jnp.concatenate([jnp.pad(vb, pad)[:, :-1], vb], axis=2)

    i = jnp.arange(BLOCK)
    c = jnp.arange(2 * BLOCK)
    d_loc = i[:, None] + BLOCK - c[None, :]
    ok_loc = (d_loc >= 0) & (d_loc < WINDOW)
    no_prev = (jnp.arange(n_blk)[:, None, None] == 0) & (c[None, None, :] < BLOCK)
    ok_loc = ok_loc[None] & ~no_prev
    bias_loc = jnp.moveaxis(head_bias(rel_bias, d_loc), -1, 0).reshape(
        A_KV_HEADS, A_GROUP, BLOCK, 2 * BLOCK)
    s_loc = jnp.einsum('bnqhgd,bnshd->bnhgqs', qb, kl) + bias_loc
    s_loc = jnp.where(ok_loc[:, None, None], s_loc, NEG)

    qpos = N_META + jnp.arange(n_blk)[:, None] * BLOCK + i[None, :]
    d_meta = qpos[..., None] - jnp.arange(N_META)
    bias_meta = head_bias(rel_bias, d_meta).reshape(
        n_blk, BLOCK, N_META, A_KV_HEADS, A_GROUP).transpose(0, 3, 4, 1, 2)
    s_meta = jnp.einsum('bnqhgd,bshd->bnhgqs', qb, km) + bias_meta

    sink_logit = sinks.astype(f32).reshape(A_KV_HEADS, A_GROUP, 1, 1)
    sink = jnp.broadcast_to(sink_logit, s_meta.shape[:-1] + (1,))
    p = jax.nn.softmax(jnp.concatenate([s_meta, s_loc, sink], axis=-1), axis=-1)
    o = (jnp.einsum('bnhgqs,bshd->bnqhgd', p[..., :N_META], vm)
         + jnp.einsum('bnhgqs,bnshd->bnqhgd', p[..., N_META:N_META + 2 * BLOCK], vl))
    o = o.reshape(b, n_blk * BLOCK, A_WIDTH)

    d_mm = jnp.arange(N_META)[:, None] - jnp.arange(N_META)[None, :]
    bias_mm = jnp.moveaxis(head_bias(rel_bias, d_mm), -1, 0).reshape(
        A_KV_HEADS, A_GROUP, N_META, N_META)
    s_mm = jnp.einsum('bqhgd,bshd->bhgqs', qm, km) + bias_mm
    s_mm = jnp.where(d_mm >= 0, s_mm, NEG)
    sink_m = jnp.broadcast_to(sink_logit, s_mm.shape[:-1] + (1,))
    p_mm = jax.nn.softmax(jnp.concatenate([s_mm, sink_m], axis=-1), axis=-1)
    o_m = jnp.einsum('bhgqs,bshd->bqhgd', p_mm[..., :N_META], vm).reshape(b, N_META, A_WIDTH)
    return jnp.concatenate([o_m, o], axis=1)


def stick_breaking_attention(q, k, v):
    f32 = jnp.float32
    b, L = q.shape[0], q.shape[1]
    n_blk = (L - N_META) // BLOCK
    q = q.reshape(b, L, B_HEADS, B_HEAD_DIM).astype(f32) * (B_HEAD_DIM ** -0.5)
    k = k.reshape(b, L, B_HEADS, B_HEAD_DIM).astype(f32)
    v = v.reshape(b, L, B_HEADS, B_HEAD_DIM).astype(f32)
    kpad = ((0, 0), (B_PAD, 0), (0, 0), (0, 0))
    kp = jnp.pad(k, kpad)
    vp = jnp.pad(v, kpad)
    kpos = jnp.arange((n_blk + 1) * BLOCK) - B_PAD
    upper = (jnp.arange(BLOCK)[:, None] >= jnp.arange(BLOCK)[None, :]).astype(f32)

    def block_out(qblk, qpos, n_kb):
        nq = qblk.shape[1]
        kk = kp[:, :n_kb * BLOCK]
        vv = vp[:, :n_kb * BLOCK]
        pos = kpos[:n_kb * BLOCK]
        z = jnp.einsum('bqhd,bshd->bhqs', qblk, kk)
        visible = (pos[None, :] >= 0) & (pos[None, :] < qpos[:, None])
        sp = jnp.where(visible, jax.nn.softplus(z), 0.0).reshape(b, B_HEADS, nq, n_kb, BLOCK)
        tw = jnp.einsum('bhqks,sj->bhqkj', sp, upper, precision=lax.Precision.HIGHEST)
        tot = tw[..., 0]
        later = lax.cumsum(tot, axis=3, reverse=True) - tot
        tail = (tw + later[..., None]).reshape(b, B_HEADS, nq, n_kb * BLOCK)
        w = jnp.exp(jnp.where(visible, z - tail, -jnp.inf))
        return jnp.einsum('bhqs,bshd->bqhd', w, vv)

    outs = [block_out(q[:, :N_META], jnp.arange(N_META), 1)]
    for n in range(n_blk):
        start = N_META + n * BLOCK
        qpos = start + jnp.arange(BLOCK)
        outs.append(block_out(q[:, start:start + BLOCK], qpos, n + 2))
    return jnp.concatenate(outs, axis=1).reshape(b, L, B_WIDTH)


def branch_a(h, w_in, sinks, w_out, rel_bias):
    z = h @ w_in
    q, k, v, gate = jnp.split(
        z, [A_WIDTH, A_WIDTH + A_KV_WIDTH, A_WIDTH + 2 * A_KV_WIDTH], axis=-1)
    o = sliding_window_sink_attention(q, k, v, sinks, rel_bias).astype(h.dtype)
    return (o * jax.nn.silu(gate)) @ w_out


def branch_b(h, w_in, w_out):
    z = h @ w_in
    q, k, v, gate = jnp.split(z, [B_WIDTH, 2 * B_WIDTH, 3 * B_WIDTH], axis=-1)
    o = stick_breaking_attention(q, k, v).astype(h.dtype)
    return (o * jax.nn.silu(gate)) @ w_out


def setup_inputs(seed: int = 0) -> dict:
    key = jax.random.key(seed)
    ks = jax.random.split(key, 10)
    f32 = jnp.float32
    nrm = jax.random.normal
    x = nrm(ks[0], (BATCH, SEQ, D_MODEL), f32)
    meta_tokens = nrm(ks[1], (N_META, D_MODEL), f32)
    rel_bias = 0.5 * nrm(ks[2], (N_BUCKETS, A_Q_HEADS), f32)
    w_in_a = nrm(ks[3], (N_A_LAYERS, D_MODEL, A_IN), f32) * D_MODEL ** -0.5
    sinks_a = 0.5 * nrm(ks[4], (N_A_LAYERS, A_Q_HEADS), f32)
    w_out_a = nrm(ks[5], (N_A_LAYERS, A_WIDTH, D_MODEL), f32) * (A_WIDTH ** -0.5 * DEEPNORM_BETA)
    w_in_b = nrm(ks[6], (N_B_LAYERS, D_MODEL, B_IN), f32) * D_MODEL ** -0.5
    w_out_b = nrm(ks[7], (N_B_LAYERS, B_WIDTH, D_MODEL), f32) * (B_WIDTH ** -0.5 * DEEPNORM_BETA)
    ln_g = 1.0 + 0.02 * nrm(ks[8], (DEPTH, D_MODEL), f32)
    ln_b = 0.02 * nrm(ks[9], (DEPTH, D_MODEL), f32)
    return {"x": x, "meta_tokens": meta_tokens, "rel_bias": rel_bias,
            "w_in_a": w_in_a, "sinks_a": sinks_a, "w_out_a": w_out_a,
            "w_in_b": w_in_b, "w_out_b": w_out_b, "ln_g": ln_g, "ln_b": ln_b}


def reference(x, meta_tokens, rel_bias, w_in_a, sinks_a, w_out_a, w_in_b, w_out_b, ln_g, ln_b):
    b = x.shape[0]
    meta = jnp.broadcast_to(meta_tokens.astype(x.dtype)[None], (b, N_META, D_MODEL))
    h = jnp.concatenate([meta, x], axis=1)
    for i in range(DEPTH):
        j = i // N_MIXERS
        if i % N_MIXERS == 0:
            y = branch_a(h, w_in_a[j], sinks_a[j], w_out_a[j], rel_bias)
        else:
            y = branch_b(h, w_in_b[j], w_out_b[j])
        h = layer_norm(DEEPNORM_ALPHA * h + y, ln_g[i], ln_b[i])
    return h[:, N_META:]
```

```python
import functools
import math

import jax
import jax.numpy as jnp
import numpy as np
from jax import lax
from jax.experimental import pallas as pl
from jax.experimental.pallas import tpu as pltpu

F32 = jnp.float32
BF16 = jnp.bfloat16

D_MODEL = 2048
SEQ = 16384
DEPTH = 4
N_META = 16
BLOCK = 128
WINDOW = 128
A_HEAD_DIM = 64
A_Q_HEADS = 32
A_KV_HEADS = 4
A_GROUP = 8
A_WIDTH = 2048
A_KV_WIDTH = 256
A_IN = 2 * A_WIDTH + 2 * A_KV_WIDTH
B_HEAD_DIM = 128
B_HEADS = 16
B_WIDTH = 2048
B_IN = 4 * B_WIDTH
N_BUCKETS = 32
MAX_DISTANCE = 128
ALPHA = (2.0 * DEPTH) ** 0.25
LN_EPS = 1e-5
NEG = -1e30

FRONT_PAD = BLOCK - N_META
L_PAD = BLOCK + SEQ + BLOCK
N_BLK = L_PAD // BLOCK

VMEM_LIMIT_BYTES = 56 * 1024 * 1024

MM_TM = 1280
LN_TM = 256
SB_T = 256


def _params(n_axes):
    return pltpu.CompilerParams(dimension_semantics=("arbitrary",) * n_axes,
                                vmem_limit_bytes=VMEM_LIMIT_BYTES)


def _mm_kernel(x_ref, w_ref, o_ref):
    o_ref[...] = jnp.dot(x_ref[...], w_ref[...],
                         preferred_element_type=F32).astype(o_ref.dtype)


def _in_proj(hb, w, tn):
    m, k = hb.shape
    n = w.shape[1]
    return pl.pallas_call(
        _mm_kernel,
        grid=(m // MM_TM, n // tn),
        in_specs=[pl.BlockSpec((MM_TM, k), lambda i, j: (i, 0)),
                  pl.BlockSpec((k, tn), lambda i, j: (0, j))],
        out_specs=pl.BlockSpec((MM_TM, tn), lambda i, j: (i, j)),
        out_shape=jax.ShapeDtypeStruct((m, n), BF16),
        name="in_proj",
        compiler_params=_params(2),
    )(hb, w)


def _out_ln_kernel(g_ref, w_ref, h_ref, gamma_ref, beta_ref, hf_ref, hb_ref):
    y = jnp.dot(g_ref[...], w_ref[...], preferred_element_type=F32)
    t = ALPHA * h_ref[...] + y
    mu = jnp.mean(t, axis=-1, keepdims=True)
    d = t - mu
    var = jnp.mean(d * d, axis=-1, keepdims=True)
    out = d * lax.rsqrt(var + LN_EPS) * gamma_ref[...] + beta_ref[...]
    hf_ref[...] = out
    hb_ref[...] = out.astype(BF16)


def _out_ln(g, w_out, h, gamma, beta):
    m, d = h.shape
    row = lambda i: (i, 0)
    fixed = lambda i: (0, 0)
    return pl.pallas_call(
        _out_ln_kernel,
        grid=(m // LN_TM,),
        in_specs=[pl.BlockSpec((LN_TM, d), row),
                  pl.BlockSpec((d, d), fixed),
                  pl.BlockSpec((LN_TM, d), row),
                  pl.BlockSpec((1, d), fixed),
                  pl.BlockSpec((1, d), fixed)],
        out_specs=[pl.BlockSpec((LN_TM, d), row), pl.BlockSpec((LN_TM, d), row)],
        out_shape=[jax.ShapeDtypeStruct((m, d), F32), jax.ShapeDtypeStruct((m, d), BF16)],
        name="out_proj_ln",
        compiler_params=_params(1),
    )(g, w_out, h, gamma, beta)


def _t5_bucket_np(dist):
    max_exact = N_BUCKETS // 2
    d = np.maximum(dist, 0)
    df = np.maximum(d, 1).astype(np.float32)
    large = max_exact + (np.log(df / np.float32(max_exact)) / np.float32(math.log(MAX_DISTANCE / max_exact))
                         * np.float32(N_BUCKETS - max_exact)).astype(np.int32)
    large = np.minimum(large, N_BUCKETS - 1)
    return np.where(d < max_exact, d, large).astype(np.int32)


def _bucket_maps():
    masked = N_BUCKETS
    r = np.arange(BLOCK)[:, None]
    c = np.arange(BLOCK)[None, :]
    out = np.full((3, BLOCK, 3 * BLOCK), masked, np.int32)
    d = r - c
    ok = (r >= FRONT_PAD) & (c >= FRONT_PAD) & (d >= 0)
    out[0, :, :BLOCK] = np.where(ok, _t5_bucket_np(d), masked)
    c2 = np.arange(2 * BLOCK)[None, :]
    d_loc = r + BLOCK - c2
    loc = np.where((d_loc >= 0) & (d_loc < WINDOW), _t5_bucket_np(d_loc), masked)
    d_meta = r + BLOCK - c
    out[1, :, :BLOCK] = np.where(c >= FRONT_PAD, _t5_bucket_np(d_meta), masked)
    out[1, :, 2 * BLOCK:] = loc[:, BLOCK:]
    far = _t5_bucket_np(np.array(2 * BLOCK + 1))
    out[2, :, :BLOCK] = np.where(c >= FRONT_PAD, far, masked) + 0 * r
    out[2, :, BLOCK:] = loc
    return out


def _bias_kernel(rel_ref, map_ref, o_ref):
    h = pl.program_id(1)
    bmap = map_ref[0]
    acc = jnp.full(bmap.shape, NEG, F32)
    for b in range(N_BUCKETS):
        acc = jnp.where(bmap == b, rel_ref[b, h], acc)
    o_ref[0, 0] = acc


def _bias_tables(rel_bias):
    maps = jnp.asarray(_bucket_maps())
    return pl.pallas_call(
        _bias_kernel,
        grid=(3, A_Q_HEADS),
        in_specs=[pl.BlockSpec(memory_space=pltpu.SMEM),
                  pl.BlockSpec((1, BLOCK, 3 * BLOCK), lambda c, h: (c, 0, 0))],
        out_specs=pl.BlockSpec((1, 1, BLOCK, 3 * BLOCK), lambda c, h: (c, h, 0, 0)),
        out_shape=jax.ShapeDtypeStruct((3, A_Q_HEADS, BLOCK, 3 * BLOCK), F32),
        name="bias_tables",
        compiler_params=_params(2),
    )(rel_bias.astype(F32), maps)


def _dup_heads(x_bf16, lane, lo_half):
    x = x_bf16.astype(F32)
    rolled = pltpu.roll(x, 64, axis=1)
    first = lane < A_HEAD_DIM
    if lo_half:
        return jnp.where(first, x, rolled).astype(BF16)
    return jnp.where(first, rolled, x).astype(BF16)


def _swa_kernel(sink_ref, q_ref, gate_ref, km_ref, kp_ref, kc_ref, vm_ref, vp_ref, vc_ref,
                bias_ref, o_ref):
    kcat = jnp.concatenate([km_ref[...], kp_ref[...], kc_ref[...]], axis=0)
    vcat = jnp.concatenate([vm_ref[...], vp_ref[...], vc_ref[...]], axis=0)
    lane_k = lax.broadcasted_iota(jnp.int32, (3 * BLOCK, BLOCK), 1)
    lane_q = lax.broadcasted_iota(jnp.int32, (BLOCK, BLOCK), 1)
    first_q = lane_q < A_HEAD_DIM
    for g in range(A_KV_HEADS):
        pair = g // 2
        kd = _dup_heads(kcat[:, pair * BLOCK:(pair + 1) * BLOCK], lane_k, g % 2 == 0)
        vd = _dup_heads(vcat[:, pair * BLOCK:(pair + 1) * BLOCK], lane_k, g % 2 == 0)
        for t in range(A_GROUP // 2):
            tile = g * (A_GROUP // 2) + t
            qt = q_ref[:, tile * BLOCK:(tile + 1) * BLOCK]
            zero = jnp.zeros_like(qt)
            qq = jnp.concatenate([jnp.where(first_q, qt, zero), jnp.where(first_q, zero, qt)], axis=0)
            s = lax.dot_general(qq, kd, (((1,), (1,)), ((), ())), preferred_element_type=F32)
            outs = []
            for e in range(2):
                head = 2 * tile + e
                se = s[e * BLOCK:(e + 1) * BLOCK] + bias_ref[0, head]
                sink = sink_ref[head]
                m = jnp.maximum(jnp.max(se, axis=-1, keepdims=True), sink)
                p = jnp.exp(se - m)
                denom = jnp.sum(p, axis=-1, keepdims=True) + jnp.exp(sink - m)
                o = jnp.dot(p.astype(BF16), vd, preferred_element_type=F32)
                outs.append(o / denom)
            o_tile = jnp.where(first_q, outs[0], outs[1])
            gate = gate_ref[:, tile * BLOCK:(tile + 1) * BLOCK].astype(F32)
            o_ref[:, tile * BLOCK:(tile + 1) * BLOCK] = (
                o_tile * (gate * jax.nn.sigmoid(gate))).astype(BF16)


def _swa(z, sinks, bias):
    kcol = 2 * A_WIDTH // A_KV_WIDTH
    vcol = kcol + 1
    prev = lambda n: jnp.maximum(n - 1, 0)
    return pl.pallas_call(
        _swa_kernel,
        grid=(N_BLK,),
        in_specs=[pl.BlockSpec(memory_space=pltpu.SMEM),
                  pl.BlockSpec((BLOCK, A_WIDTH), lambda n: (n, 0)),
                  pl.BlockSpec((BLOCK, A_WIDTH), lambda n: (n, 1)),
                  pl.BlockSpec((BLOCK, A_KV_WIDTH), lambda n: (0, kcol)),
                  pl.BlockSpec((BLOCK, A_KV_WIDTH), lambda n: (prev(n), kcol)),
                  pl.BlockSpec((BLOCK, A_KV_WIDTH), lambda n: (n, kcol)),
                  pl.BlockSpec((BLOCK, A_KV_WIDTH), lambda n: (0, vcol)),
                  pl.BlockSpec((BLOCK, A_KV_WIDTH), lambda n: (prev(n), vcol)),
                  pl.BlockSpec((BLOCK, A_KV_WIDTH), lambda n: (n, vcol)),
                  pl.BlockSpec((1, A_Q_HEADS, BLOCK, 3 * BLOCK),
                               lambda n: (jnp.minimum(n, 2), 0, 0, 0))],
        out_specs=pl.BlockSpec((BLOCK, A_WIDTH), lambda n: (n, 0)),
        out_shape=jax.ShapeDtypeStruct((L_PAD, A_WIDTH), BF16),
        name="swa_mixer",
        compiler_params=_params(1),
    )(sinks.astype(F32), z, z, z, z, z, z, z, z, bias)


def _softplus(s):
    return jnp.maximum(s, 0.0) + jnp.log(1.0 + jnp.exp(-jnp.abs(s)))


def _sb_kernel(q_ref, k_ref, v_ref, gate_ref, o_ref):
    i = pl.program_id(1)
    q = q_ref[...]
    r_iota = lax.broadcasted_iota(jnp.int32, (SB_T, SB_T), 0)
    c_iota = lax.broadcasted_iota(jnp.int32, (SB_T, SB_T), 1)
    suffix = (r_iota >= c_iota).astype(BF16)
    row = i * SB_T + r_iota

    def body(t, carry):
        acc, later = carry
        j = i - t
        start = pl.multiple_of(j * SB_T, SB_T)
        k = k_ref[pl.ds(start, SB_T), :]
        v = v_ref[pl.ds(start, SB_T), :]
        s = lax.dot_general(q, k, (((1,), (1,)), ((), ())), preferred_element_type=F32)
        col = j * SB_T + c_iota
        vis = (col >= FRONT_PAD) & (col < row)
        sp = jnp.where(vis, _softplus(s), 0.0)
        tw = jnp.dot(sp.astype(BF16), suffix, preferred_element_type=F32)
        w = jnp.where(vis, jnp.exp(s - (tw + later)), 0.0)
        acc = acc + jnp.dot(w.astype(BF16), v, preferred_element_type=F32)
        return acc, later + tw[:, :1]

    acc, _ = lax.fori_loop(0, i + 1, body,
                           (jnp.zeros((SB_T, B_HEAD_DIM), F32), jnp.zeros((SB_T, 1), F32)))
    gate = gate_ref[...].astype(F32)
    o_ref[...] = (acc * (gate * jax.nn.sigmoid(gate))).astype(BF16)


def _stick_breaking(z):
    hcols = B_WIDTH // B_HEAD_DIM
    return pl.pallas_call(
        _sb_kernel,
        grid=(B_HEADS, L_PAD // SB_T),
        in_specs=[pl.BlockSpec((SB_T, B_HEAD_DIM), lambda h, i: (i, h)),
                  pl.BlockSpec((L_PAD, B_HEAD_DIM), lambda h, i: (0, hcols + h)),
                  pl.BlockSpec((L_PAD, B_HEAD_DIM), lambda h, i: (0, 2 * hcols + h)),
                  pl.BlockSpec((SB_T, B_HEAD_DIM), lambda h, i: (i, 3 * hcols + h))],
        out_specs=pl.BlockSpec((SB_T, B_HEAD_DIM), lambda h, i: (i, h)),
        out_shape=jax.ShapeDtypeStruct((L_PAD, B_WIDTH), BF16),
        name="stick_breaking_mixer",
        compiler_params=_params(2),
    )(z, z, z, z)


def _prep_w_in_a(w):
    q, k, v, gate = jnp.split(w, [A_WIDTH, A_WIDTH + A_KV_WIDTH, A_WIDTH + 2 * A_KV_WIDTH], axis=-1)
    return jnp.concatenate([q * (A_HEAD_DIM ** -0.5), gate, k, v], axis=-1).astype(BF16)


def _prep_w_in_b(w):
    return jnp.concatenate([w[:, :B_WIDTH] * (B_HEAD_DIM ** -0.5), w[:, B_WIDTH:]], axis=-1).astype(BF16)


def _forward_one(x, meta_tokens, rel_bias, w_in_a, sinks_a, w_out_a, w_in_b, w_out_b, ln_g, ln_b):
    h = jnp.concatenate([jnp.zeros((FRONT_PAD, D_MODEL), F32), meta_tokens.astype(F32), x.astype(F32),
                         jnp.zeros((BLOCK, D_MODEL), F32)], axis=0)
    hb = h.astype(BF16)
    bias = _bias_tables(rel_bias)
    for i in range(DEPTH):
        j = i // 2
        if i % 2 == 0:
            z = _in_proj(hb, _prep_w_in_a(w_in_a[j]), 1536)
            g = _swa(z, sinks_a[j], bias)
            w_out = w_out_a[j]
        else:
            z = _in_proj(hb, _prep_w_in_b(w_in_b[j]), 1024)
            g = _stick_breaking(z)
            w_out = w_out_b[j]
        h, hb = _out_ln(g, w_out.astype(BF16), h, ln_g[i][None, :].astype(F32), ln_b[i][None, :].astype(F32))
    return h[BLOCK:BLOCK + SEQ]


def kernel(x, meta_tokens, rel_bias, w_in_a, sinks_a, w_out_a, w_in_b, w_out_b, ln_g, ln_b):
    assert x.shape[1:] == (SEQ, D_MODEL)
    outs = [_forward_one(x[b], meta_tokens, rel_bias, w_in_a, sinks_a, w_out_a, w_in_b, w_out_b, ln_g, ln_b)
            for b in range(x.shape[0])]
    return jnp.stack(outs, axis=0).astype(x.dtype)
```

```python
import functools
import math

import jax
import jax.numpy as jnp
import numpy as np
from jax import lax
from jax.experimental import pallas as pl
from jax.experimental.pallas import tpu as pltpu

F32 = jnp.float32
BF16 = jnp.bfloat16

D_MODEL = 2048
SEQ = 16384
DEPTH = 4
N_META = 16
BLOCK = 128
WINDOW = 128
A_HEAD_DIM = 64
A_Q_HEADS = 32
A_KV_HEADS = 4
A_GROUP = 8
A_WIDTH = 2048
A_KV_WIDTH = 256
A_IN = 2 * A_WIDTH + 2 * A_KV_WIDTH
B_HEAD_DIM = 128
B_HEADS = 16
B_WIDTH = 2048
B_IN = 4 * B_WIDTH
N_BUCKETS = 32
MAX_DISTANCE = 128
ALPHA = (2.0 * DEPTH) ** 0.25
LN_EPS = 1e-5
NEG = -1e30

FRONT_PAD = BLOCK - N_META
TAIL_PAD = 3 * BLOCK
L_PAD = BLOCK + SEQ + TAIL_PAD
N_BLK = L_PAD // BLOCK

VMEM_LIMIT_BYTES = 56 * 1024 * 1024

MM_TM = 1536
LN_TM = 256
SB_TQ = 512
SB_TK = 256
SB_HG = 2
SB_DEAD = 152.0
LOG2E = math.log2(math.e)


def _params(n_axes):
    return pltpu.CompilerParams(dimension_semantics=("arbitrary",) * n_axes,
                                vmem_limit_bytes=VMEM_LIMIT_BYTES)


def _mm_kernel(x_ref, w_ref, o_ref):
    o_ref[...] = jnp.dot(x_ref[...], w_ref[...],
                         preferred_element_type=F32).astype(o_ref.dtype)


def _in_proj(hb, w, tn):
    m, k = hb.shape
    n = w.shape[1]
    return pl.pallas_call(
        _mm_kernel,
        grid=(m // MM_TM, n // tn),
        in_specs=[pl.BlockSpec((MM_TM, k), lambda i, j: (i, 0)),
                  pl.BlockSpec((k, tn), lambda i, j: (0, j))],
        out_specs=pl.BlockSpec((MM_TM, tn), lambda i, j: (i, j)),
        out_shape=jax.ShapeDtypeStruct((m, n), BF16),
        name="in_proj",
        compiler_params=_params(2),
    )(hb, w)


def _out_ln_kernel(g_ref, w_ref, h_ref, gamma_ref, beta_ref, hf_ref, hb_ref):
    y = jnp.dot(g_ref[...], w_ref[...], preferred_element_type=F32)
    t = ALPHA * h_ref[...] + y
    mu = jnp.mean(t, axis=-1, keepdims=True)
    d = t - mu
    var = jnp.mean(d * d, axis=-1, keepdims=True)
    out = d * lax.rsqrt(var + LN_EPS) * gamma_ref[...] + beta_ref[...]
    hf_ref[...] = out
    hb_ref[...] = out.astype(BF16)


def _out_ln(g, w_out, h, gamma, beta):
    m, d = h.shape
    row = lambda i: (i, 0)
    fixed = lambda i: (0, 0)
    return pl.pallas_call(
        _out_ln_kernel,
        grid=(m // LN_TM,),
        in_specs=[pl.BlockSpec((LN_TM, d), row),
                  pl.BlockSpec((d, d), fixed),
                  pl.BlockSpec((LN_TM, d), row),
                  pl.BlockSpec((1, d), fixed),
                  pl.BlockSpec((1, d), fixed)],
        out_specs=[pl.BlockSpec((LN_TM, d), row), pl.BlockSpec((LN_TM, d), row)],
        out_shape=[jax.ShapeDtypeStruct((m, d), F32), jax.ShapeDtypeStruct((m, d), BF16)],
        name="out_proj_ln",
        compiler_params=_params(1),
    )(g, w_out, h, gamma, beta)


def _t5_bucket_np(dist):
    max_exact = N_BUCKETS // 2
    d = np.maximum(dist, 0)
    df = np.maximum(d, 1).astype(np.float32)
    large = max_exact + (np.log(df / np.float32(max_exact)) / np.float32(math.log(MAX_DISTANCE / max_exact))
                         * np.float32(N_BUCKETS - max_exact)).astype(np.int32)
    large = np.minimum(large, N_BUCKETS - 1)
    return np.where(d < max_exact, d, large).astype(np.int32)


def _bucket_maps():
    masked = N_BUCKETS
    r = np.arange(BLOCK)[:, None]
    c = np.arange(BLOCK)[None, :]
    out = np.full((3, BLOCK, 3 * BLOCK), masked, np.int32)
    d = r - c
    ok = (r >= FRONT_PAD) & (c >= FRONT_PAD) & (d >= 0)
    out[0, :, :BLOCK] = np.where(ok, _t5_bucket_np(d), masked)
    c2 = np.arange(2 * BLOCK)[None, :]
    d_loc = r + BLOCK - c2
    loc = np.where((d_loc >= 0) & (d_loc < WINDOW), _t5_bucket_np(d_loc), masked)
    d_meta = r + BLOCK - c
    out[1, :, :BLOCK] = np.where(c >= FRONT_PAD, _t5_bucket_np(d_meta), masked)
    out[1, :, 2 * BLOCK:] = loc[:, BLOCK:]
    far = _t5_bucket_np(np.array(2 * BLOCK + 1))
    out[2, :, :BLOCK] = np.where(c >= FRONT_PAD, far, masked) + 0 * r
    out[2, :, BLOCK:] = loc
    return out


def _bias_kernel(rel_ref, map_ref, o_ref):
    h = pl.program_id(1)
    bmap = map_ref[0]
    acc = jnp.full(bmap.shape, NEG, F32)
    for b in range(N_BUCKETS):
        acc = jnp.where(bmap == b, rel_ref[b, h], acc)
    o_ref[0, 0] = acc


def _bias_tables(rel_bias):
    maps = jnp.asarray(_bucket_maps())
    return pl.pallas_call(
        _bias_kernel,
        grid=(3, A_Q_HEADS),
        in_specs=[pl.BlockSpec(memory_space=pltpu.SMEM),
                  pl.BlockSpec((1, BLOCK, 3 * BLOCK), lambda c, h: (c, 0, 0))],
        out_specs=pl.BlockSpec((1, 1, BLOCK, 3 * BLOCK), lambda c, h: (c, h, 0, 0)),
        out_shape=jax.ShapeDtypeStruct((3, A_Q_HEADS, BLOCK, 3 * BLOCK), F32),
        name="bias_tables",
        compiler_params=_params(2),
    )(rel_bias.astype(F32), maps)


def _dup_heads(x_bf16, lane, lo_half):
    x = x_bf16.astype(F32)
    rolled = pltpu.roll(x, 64, axis=1)
    first = lane < A_HEAD_DIM
    if lo_half:
        return jnp.where(first, x, rolled).astype(BF16)
    return jnp.where(first, rolled, x).astype(BF16)


def _swa_kernel(sink_ref, q_ref, gate_ref, km_ref, kp_ref, kc_ref, vm_ref, vp_ref, vc_ref,
                bias_ref, o_ref):
    kcat = jnp.concatenate([km_ref[...], kp_ref[...], kc_ref[...]], axis=0)
    vcat = jnp.concatenate([vm_ref[...], vp_ref[...], vc_ref[...]], axis=0)
    lane_k = lax.broadcasted_iota(jnp.int32, (3 * BLOCK, BLOCK), 1)
    lane_q = lax.broadcasted_iota(jnp.int32, (BLOCK, BLOCK), 1)
    first_q = lane_q < A_HEAD_DIM
    for g in range(A_KV_HEADS):
        pair = g // 2
        kd = _dup_heads(kcat[:, pair * BLOCK:(pair + 1) * BLOCK], lane_k, g % 2 == 0)
        vd = _dup_heads(vcat[:, pair * BLOCK:(pair + 1) * BLOCK], lane_k, g % 2 == 0)
        for t in range(A_GROUP // 2):
            tile = g * (A_GROUP // 2) + t
            qt = q_ref[:, tile * BLOCK:(tile + 1) * BLOCK]
            zero = jnp.zeros_like(qt)
            qq = jnp.concatenate([jnp.where(first_q, qt, zero), jnp.where(first_q, zero, qt)], axis=0)
            s = lax.dot_general(qq, kd, (((1,), (1,)), ((), ())), preferred_element_type=F32)
            outs = []
            for e in range(2):
                head = 2 * tile + e
                se = s[e * BLOCK:(e + 1) * BLOCK] + bias_ref[0, head]
                sink = sink_ref[head]
                m = jnp.maximum(jnp.max(se, axis=-1, keepdims=True), sink)
                p = jnp.exp(se - m)
                denom = jnp.sum(p, axis=-1, keepdims=True) + jnp.exp(sink - m)
                o = jnp.dot(p.astype(BF16), vd, preferred_element_type=F32)
                outs.append(o / denom)
            o_tile = jnp.where(first_q, outs[0], outs[1])
            gate = gate_ref[:, tile * BLOCK:(tile + 1) * BLOCK].astype(F32)
            o_ref[:, tile * BLOCK:(tile + 1) * BLOCK] = (
                o_tile * (gate * jax.nn.sigmoid(gate))).astype(BF16)


def _swa(z, sinks, bias):
    kcol = 2 * A_WIDTH // A_KV_WIDTH
    vcol = kcol + 1
    prev = lambda n: jnp.maximum(n - 1, 0)
    return pl.pallas_call(
        _swa_kernel,
        grid=(N_BLK,),
        in_specs=[pl.BlockSpec(memory_space=pltpu.SMEM),
                  pl.BlockSpec((BLOCK, A_WIDTH), lambda n: (n, 0)),
                  pl.BlockSpec((BLOCK, A_WIDTH), lambda n: (n, 1)),
                  pl.BlockSpec((BLOCK, A_KV_WIDTH), lambda n: (0, kcol)),
                  pl.BlockSpec((BLOCK, A_KV_WIDTH), lambda n: (prev(n), kcol)),
                  pl.BlockSpec((BLOCK, A_KV_WIDTH), lambda n: (n, kcol)),
                  pl.BlockSpec((BLOCK, A_KV_WIDTH), lambda n: (0, vcol)),
                  pl.BlockSpec((BLOCK, A_KV_WIDTH), lambda n: (prev(n), vcol)),
                  pl.BlockSpec((BLOCK, A_KV_WIDTH), lambda n: (n, vcol)),
                  pl.BlockSpec((1, A_Q_HEADS, BLOCK, 3 * BLOCK),
                               lambda n: (jnp.minimum(n, 2), 0, 0, 0))],
        out_specs=pl.BlockSpec((BLOCK, A_WIDTH), lambda n: (n, 0)),
        out_shape=jax.ShapeDtypeStruct((L_PAD, A_WIDTH), BF16),
        name="swa_mixer",
        compiler_params=_params(1),
    )(sinks.astype(F32), z, z, z, z, z, z, z, z, bias)


def _softplus2(s):
    neg_abs = lax.bitcast_convert_type(lax.bitcast_convert_type(s, jnp.uint32) | jnp.uint32(0x80000000), F32)
    return jnp.maximum(s, 0.0) + jnp.log2(1.0 + jnp.exp2(neg_abs))


def _sb_kernel(q_ref, k_ref, v_ref, gate_ref, o_ref, acc_ref, later_ref):
    qi = pl.program_id(1)
    r_iota = lax.broadcasted_iota(jnp.int32, (SB_TK, SB_TK), 0)
    c_iota = lax.broadcasted_iota(jnp.int32, (SB_TK, SB_TK), 1)
    suffix = (r_iota >= c_iota).astype(BF16)
    acc_ref[...] = jnp.zeros_like(acc_ref)
    later_ref[...] = jnp.zeros_like(later_ref)

    def tile(j, masked):
        start = pl.multiple_of(j * SB_TK, SB_TK)
        if masked:
            row = qi * SB_TQ + lax.broadcasted_iota(jnp.int32, (SB_TQ, SB_TK), 0)
            col = j * SB_TK + lax.broadcasted_iota(jnp.int32, (SB_TQ, SB_TK), 1)
            vis = (col >= FRONT_PAD) & (col < row)
        for hh in range(SB_HG):
            lanes = slice(hh * B_HEAD_DIM, (hh + 1) * B_HEAD_DIM)
            q = q_ref[:, lanes]
            k = k_ref[pl.ds(start, SB_TK), lanes]
            v = v_ref[pl.ds(start, SB_TK), lanes]
            s = lax.dot_general(q, k, (((1,), (1,)), ((), ())), preferred_element_type=F32)
            sp = _softplus2(s)
            if masked:
                sp = jnp.where(vis, sp, 0.0)
            tw = jnp.dot(sp.astype(BF16), suffix, preferred_element_type=F32)
            later = later_ref[hh]
            p = jnp.exp2(s - tw - jnp.concatenate([later] * (SB_TK // B_HEAD_DIM), axis=1))
            if masked:
                p = jnp.where(vis, p, 0.0)
            acc_ref[hh] += jnp.dot(p.astype(BF16), v, preferred_element_type=F32)
            tot = jnp.sum(sp, axis=-1, keepdims=True)
            later_ref[hh] = later + jnp.broadcast_to(tot, later.shape)

    def all_dead():
        return jnp.min(later_ref[...]) >= SB_DEAD

    ratio = SB_TQ // SB_TK
    for d in range(ratio):
        tile(qi * ratio + (ratio - 1 - d), True)

    def body(carry):
        j, _ = carry
        tile(j, False)
        return j - 1, all_dead()

    _, dead = lax.while_loop(lambda c: (c[0] >= 1) & jnp.logical_not(c[1]), body,
                             (qi * ratio - 1, all_dead()))

    @pl.when((qi >= 1) & jnp.logical_not(dead))
    def _():
        tile(0, True)

    for hh in range(SB_HG):
        lanes = slice(hh * B_HEAD_DIM, (hh + 1) * B_HEAD_DIM)
        gate = gate_ref[:, lanes].astype(F32)
        o_ref[:, lanes] = (acc_ref[hh] * (gate * jax.nn.sigmoid(gate))).astype(BF16)


def _stick_breaking(z):
    width = SB_HG * B_HEAD_DIM
    hcols = B_WIDTH // width
    return pl.pallas_call(
        _sb_kernel,
        grid=(B_HEADS // SB_HG, L_PAD // SB_TQ),
        in_specs=[pl.BlockSpec((SB_TQ, width), lambda h, i: (i, h)),
                  pl.BlockSpec((L_PAD, width), lambda h, i: (0, hcols + h)),
                  pl.BlockSpec((L_PAD, width), lambda h, i: (0, 2 * hcols + h)),
                  pl.BlockSpec((SB_TQ, width), lambda h, i: (i, 3 * hcols + h))],
        out_specs=pl.BlockSpec((SB_TQ, width), lambda h, i: (i, h)),
        out_shape=jax.ShapeDtypeStruct((L_PAD, B_WIDTH), BF16),
        scratch_shapes=[pltpu.VMEM((SB_HG, SB_TQ, B_HEAD_DIM), F32),
                        pltpu.VMEM((SB_HG, SB_TQ, B_HEAD_DIM), F32)],
        name="stick_breaking_mixer",
        compiler_params=_params(2),
    )(z, z, z, z)


def _prep_w_in_a(w):
    q, k, v, gate = jnp.split(w, [A_WIDTH, A_WIDTH + A_KV_WIDTH, A_WIDTH + 2 * A_KV_WIDTH], axis=-1)
    return jnp.concatenate([q * (A_HEAD_DIM ** -0.5), gate, k, v], axis=-1).astype(BF16)


def _prep_w_in_b(w):
    col = jnp.arange(B_IN)[None, :]
    scale = jnp.where(col < B_WIDTH, B_HEAD_DIM ** -0.5 * LOG2E, 1.0).astype(F32)
    return (w * scale).astype(BF16)


def _forward_one(x, meta_tokens, rel_bias, w_in_a, sinks_a, w_out_a, w_in_b, w_out_b, ln_g, ln_b):
    h = jnp.concatenate([jnp.zeros((FRONT_PAD, D_MODEL), F32), meta_tokens.astype(F32), x.astype(F32),
                         jnp.zeros((TAIL_PAD, D_MODEL), F32)], axis=0)
    hb = h.astype(BF16)
    bias = _bias_tables(rel_bias)
    for i in range(DEPTH):
        j = i // 2
        if i % 2 == 0:
            z = _in_proj(hb, _prep_w_in_a(w_in_a[j]), 1536)
            g = _swa(z, sinks_a[j], bias)
            w_out = w_out_a[j]
        else:
            z = _in_proj(hb, _prep_w_in_b(w_in_b[j]), 1024)
            g = _stick_breaking(z)
            w_out = w_out_b[j]
        h, hb = _out_ln(g, w_out.astype(BF16), h, ln_g[i][None, :].astype(F32), ln_b[i][None, :].astype(F32))
    return h[BLOCK:BLOCK + SEQ]


def kernel(x, meta_tokens, rel_bias, w_in_a, sinks_a, w_out_a, w_in_b, w_out_b, ln_g, ln_b):
    assert x.shape[1:] == (SEQ, D_MODEL)
    outs = [_forward_one(x[b], meta_tokens, rel_bias, w_in_a, sinks_a, w_out_a, w_in_b, w_out_b, ln_g, ln_b)
            for b in range(x.shape[0])]
    return jnp.stack(outs, axis=0).astype(x.dtype)
```

```python
import functools
import math

import jax
import jax.numpy as jnp
import numpy as np
from jax import lax
from jax.experimental import pallas as pl
from jax.experimental.pallas import tpu as pltpu

F32 = jnp.float32
BF16 = jnp.bfloat16

D_MODEL = 2048
SEQ = 16384
DEPTH = 4
N_META = 16
BLOCK = 128
WINDOW = 128
A_HEAD_DIM = 64
A_Q_HEADS = 32
A_KV_HEADS = 4
A_GROUP = 8
A_WIDTH = 2048
A_KV_WIDTH = 256
A_IN = 2 * A_WIDTH + 2 * A_KV_WIDTH
B_HEAD_DIM = 128
B_HEADS = 16
B_WIDTH = 2048
B_IN = 4 * B_WIDTH
N_BUCKETS = 32
MAX_DISTANCE = 128
ALPHA = (2.0 * DEPTH) ** 0.25
LN_EPS = 1e-5
NEG = -1e30
LOG2E = math.log2(math.e)

FRONT = 512
META_ROW0 = FRONT - N_META
L_PAD = FRONT + SEQ
N_BLK = L_PAD // BLOCK
META_BLK = FRONT // BLOCK - 1
META_LOCAL = BLOCK - N_META

VMEM_LIMIT_BYTES = 56 * 1024 * 1024

LN_TM = 256
SB_TQ = 512
SB_TK = 256
SB_HG = 2
SB_DEAD = 152.0


def _params(n_axes):
    return pltpu.CompilerParams(dimension_semantics=("arbitrary",) * n_axes,
                                vmem_limit_bytes=VMEM_LIMIT_BYTES)


def _cast_weights(w_ref, scale_ref, wb_ref):
    @pl.when(pl.program_id(1) == 0)
    def _():
        wb_ref[...] = (w_ref[...] * scale_ref[...]).astype(BF16)


def _in_proj_kernel(x_ref, w_ref, scale_ref, o_ref, wb_ref):
    _cast_weights(w_ref, scale_ref, wb_ref)
    o_ref[...] = jnp.dot(x_ref[...], wb_ref[...], preferred_element_type=F32).astype(o_ref.dtype)


def _in_proj_first_kernel(head_ref, x_ref, w_ref, scale_ref, o_ref, wb_ref):
    _cast_weights(w_ref, scale_ref, wb_ref)
    x = jnp.where(pl.program_id(1) == 0, head_ref[...], x_ref[...]).astype(BF16)
    o_ref[...] = jnp.dot(x, wb_ref[...], preferred_element_type=F32).astype(o_ref.dtype)


def _in_proj(x, w, scale, tm, tn, head=None):
    k, n = w.shape
    w_spec = pl.BlockSpec((k, tn), lambda j, i: (0, j))
    s_spec = pl.BlockSpec((1, tn), lambda j, i: (0, j))
    if head is None:
        kern = _in_proj_kernel
        in_specs = [pl.BlockSpec((tm, k), lambda j, i: (i, 0)), w_spec, s_spec]
        args = (x, w, scale)
    else:
        assert tm == FRONT
        kern = _in_proj_first_kernel
        in_specs = [pl.BlockSpec((tm, k), lambda j, i: (0, 0)),
                    pl.BlockSpec((tm, k), lambda j, i: (jnp.maximum(i - 1, 0), 0)), w_spec, s_spec]
        args = (head, x, w, scale)
    return pl.pallas_call(
        kern,
        grid=(n // tn, L_PAD // tm),
        in_specs=in_specs,
        out_specs=pl.BlockSpec((tm, tn), lambda j, i: (i, j)),
        out_shape=jax.ShapeDtypeStruct((L_PAD, n), BF16),
        scratch_shapes=[pltpu.VMEM((k, tn), BF16)],
        name="in_proj",
        compiler_params=_params(2),
    )(*args)


def _deepnorm(g_ref, w_ref, h, gamma_ref, beta_ref):
    y = jnp.dot(g_ref[...], w_ref[...], preferred_element_type=F32)
    t = ALPHA * h + y
    mu = jnp.mean(t, axis=-1, keepdims=True)
    d = t - mu
    var = jnp.mean(d * d, axis=-1, keepdims=True)
    return d * lax.rsqrt(var + LN_EPS) * gamma_ref[...] + beta_ref[...]


def _out_ln_first_kernel(g_ref, w_ref, head_ref, x_ref, gamma_ref, beta_ref, hf_ref, hb_ref):
    h = jnp.where(pl.program_id(0) < FRONT // LN_TM, head_ref[...], x_ref[...])
    out = _deepnorm(g_ref, w_ref, h, gamma_ref, beta_ref)
    hf_ref[...] = out
    hb_ref[...] = out.astype(BF16)


def _out_ln_mid_kernel(g_ref, w_ref, h_ref, gamma_ref, beta_ref, hf_ref, hb_ref):
    out = _deepnorm(g_ref, w_ref, h_ref[...], gamma_ref, beta_ref)
    hf_ref[...] = out
    hb_ref[...] = out.astype(BF16)


def _out_ln_last_kernel(g_ref, w_ref, h_ref, gamma_ref, beta_ref, o_ref):
    o_ref[...] = _deepnorm(g_ref, w_ref, h_ref[...], gamma_ref, beta_ref)


def _out_ln(g, w_out, gamma, beta, *, h=None, head=None, x=None, last=False):
    d = D_MODEL
    front_tiles = FRONT // LN_TM
    row = lambda i: (i, 0)
    fixed = lambda i: (0, 0)
    tok = lambda i: (jnp.maximum(i - front_tiles, 0), 0)
    tile = lambda imap: pl.BlockSpec((LN_TM, d), imap)
    common = [tile(row), pl.BlockSpec((d, d), fixed)]
    tail = [pl.BlockSpec((1, d), fixed), pl.BlockSpec((1, d), fixed)]
    both = dict(out_specs=[tile(row), tile(row)],
                out_shape=[jax.ShapeDtypeStruct((L_PAD, d), F32), jax.ShapeDtypeStruct((L_PAD, d), BF16)])
    if head is not None:
        kern, mid, args = _out_ln_first_kernel, [tile(lambda i: (jnp.minimum(i, front_tiles - 1), 0)), tile(tok)], (head, x)
        outs = both
    elif last:
        kern, mid, args = _out_ln_last_kernel, [tile(row)], (h,)
        outs = dict(out_specs=tile(tok), out_shape=jax.ShapeDtypeStruct((SEQ, d), F32))
    else:
        kern, mid, args = _out_ln_mid_kernel, [tile(row)], (h,)
        outs = both
    return pl.pallas_call(
        kern,
        grid=(L_PAD // LN_TM,),
        in_specs=common + mid + tail,
        name="out_proj_ln",
        compiler_params=_params(1),
        **outs,
    )(g, w_out, *args, gamma, beta)


def _t5_bucket_np(dist):
    max_exact = N_BUCKETS // 2
    d = np.maximum(dist, 0)
    df = np.maximum(d, 1).astype(np.float32)
    large = max_exact + (np.log(df / np.float32(max_exact)) / np.float32(math.log(MAX_DISTANCE / max_exact))
                         * np.float32(N_BUCKETS - max_exact)).astype(np.int32)
    large = np.minimum(large, N_BUCKETS - 1)
    return np.where(d < max_exact, d, large).astype(np.int32)


def _bucket_maps():
    masked = N_BUCKETS
    r = np.arange(BLOCK)[:, None]
    c = np.arange(BLOCK)[None, :]
    out = np.full((3, BLOCK, 2 * BLOCK), masked, np.int32)
    ok = (r >= META_LOCAL) & (c >= META_LOCAL) & (r - c >= 0)
    out[0, :, :BLOCK] = np.where(ok, _t5_bucket_np(r - c), masked)
    own = _t5_bucket_np(r - c)
    prev = _t5_bucket_np(r + BLOCK - c)
    out[1, :, :BLOCK] = np.where(c >= META_LOCAL, _t5_bucket_np(r + BLOCK - c), masked)
    out[1, :, BLOCK:] = np.where(c <= r, own, masked)
    far = _t5_bucket_np(np.array(2 * BLOCK + 1))
    out[2, :, :BLOCK] = np.where(c >= META_LOCAL, far, masked) + 0 * r
    out[2, :, BLOCK:] = np.where(c <= r, own, prev)
    return out


def _bias_kernel(rel_ref, map_ref, o_ref):
    bmap = map_ref[0]

    def one_head(h, carry):
        acc = jnp.full(bmap.shape, NEG, F32)
        for b in range(N_BUCKETS):
            acc = jnp.where(bmap == b, rel_ref[b, h], acc)
        o_ref[0, h] = acc
        return carry

    lax.fori_loop(0, A_Q_HEADS, one_head, 0)


def _bias_tables(rel_bias):
    maps = jnp.asarray(_bucket_maps())
    return pl.pallas_call(
        _bias_kernel,
        grid=(3,),
        in_specs=[pl.BlockSpec(memory_space=pltpu.SMEM),
                  pl.BlockSpec((1, BLOCK, 2 * BLOCK), lambda c: (c, 0, 0))],
        out_specs=pl.BlockSpec((1, A_Q_HEADS, BLOCK, 2 * BLOCK), lambda c: (c, 0, 0, 0)),
        out_shape=jax.ShapeDtypeStruct((3, A_Q_HEADS, BLOCK, 2 * BLOCK), F32),
        name="bias_tables",
        compiler_params=_params(1),
    )(rel_bias.astype(F32), maps)


def _swa_kernel(sink_ref, zc_ref, zp_ref, zm_ref, bias_ref, o_ref):
    kv0 = A_WIDTH
    kcat = jnp.concatenate([zm_ref[:, :A_KV_WIDTH], zp_ref[:, :A_KV_WIDTH],
                            zc_ref[:, kv0:kv0 + A_KV_WIDTH]], axis=0)
    vcat = jnp.concatenate([zm_ref[:, A_KV_WIDTH:], zp_ref[:, A_KV_WIDTH:],
                            zc_ref[:, kv0 + A_KV_WIDTH:kv0 + 2 * A_KV_WIDTH]], axis=0)
    gate0 = kv0 + 2 * A_KV_WIDTH
    lane_k = lax.broadcasted_iota(jnp.int32, (3 * BLOCK, BLOCK), 1) < A_HEAD_DIM
    first = lax.broadcasted_iota(jnp.int32, (BLOCK, BLOCK), 1) < A_HEAD_DIM
    own = (lax.broadcasted_iota(jnp.int32, (BLOCK, BLOCK), 0)
           >= lax.broadcasted_iota(jnp.int32, (BLOCK, BLOCK), 1))
    tiles_per_group = A_GROUP // 2
    n_tiles = A_KV_HEADS * tiles_per_group
    kv_cache = {}

    def kv_of(g):
        if g not in kv_cache:
            pair = slice((g // 2) * BLOCK, (g // 2 + 1) * BLOCK)
            k2 = kcat[:, pair].astype(F32)
            v2 = vcat[:, pair].astype(F32)
            k2r = pltpu.roll(k2, A_HEAD_DIM, axis=1)
            v2r = pltpu.roll(v2, A_HEAD_DIM, axis=1)
            if g % 2 == 0:
                kd, v_lo, v_hi = jnp.where(lane_k, k2, k2r), v2, v2r
            else:
                kd, v_lo, v_hi = jnp.where(lane_k, k2r, k2), v2r, v2
            vd = [jnp.where(lane_k, v_lo, 1.0).astype(BF16), jnp.where(lane_k, 1.0, v_hi).astype(BF16)]
            kv_cache[g] = (kd.astype(BF16), vd)
        return kv_cache[g]

    def scores(tile):
        qt = zc_ref[:, tile * BLOCK:(tile + 1) * BLOCK]
        zero = jnp.zeros_like(qt)
        qq = jnp.concatenate([jnp.where(first, qt, zero), jnp.where(first, zero, qt)], axis=0)
        kd, _ = kv_of(tile // tiles_per_group)
        return lax.dot_general(qq, kd, (((1,), (1,)), ((), ())), preferred_element_type=F32)

    s_next = scores(0)
    for tile in range(n_tiles):
        s = s_next
        if tile + 1 < n_tiles:
            s_next = scores(tile + 1)
        _, vd = kv_of(tile // tiles_per_group)
        cols = slice(tile * BLOCK, (tile + 1) * BLOCK)
        res, sink_term = [], []
        for e in range(2):
            head = 2 * tile + e
            se = s[e * BLOCK:(e + 1) * BLOCK]
            window = jnp.where(own, se[:, 2 * BLOCK:], se[:, BLOCK:2 * BLOCK])
            sc = jnp.concatenate([se[:, :BLOCK], window], axis=1) + bias_ref[0, head]
            sink = sink_ref[head]
            m = jnp.maximum(jnp.max(sc, axis=-1, keepdims=True), sink)
            p = jnp.exp(sc - m)
            pw = p[:, BLOCK:]
            pf = jnp.concatenate([p[:, :BLOCK], jnp.where(own, 0.0, pw), jnp.where(own, pw, 0.0)],
                                 axis=1).astype(BF16)
            res.append(jnp.dot(pf, vd[e], preferred_element_type=F32))
            sink_term.append(jnp.exp(sink - m))
        num = jnp.where(first, res[0], res[1])
        den = (pltpu.roll(jnp.where(first, res[1], res[0]), A_HEAD_DIM, axis=1)
               + jnp.where(first, sink_term[0], sink_term[1]))
        gate = zc_ref[:, gate0 + tile * BLOCK:gate0 + (tile + 1) * BLOCK].astype(F32)
        o_ref[:, cols] = (num / den * (gate * jax.nn.sigmoid(gate))).astype(BF16)


def _swa(z, sinks, bias):
    kv_tile = A_WIDTH // (2 * A_KV_WIDTH)
    return pl.pallas_call(
        _swa_kernel,
        grid=(N_BLK,),
        in_specs=[pl.BlockSpec(memory_space=pltpu.SMEM),
                  pl.BlockSpec((BLOCK, A_IN), lambda n: (n, 0)),
                  pl.BlockSpec((BLOCK, 2 * A_KV_WIDTH), lambda n: (jnp.maximum(n - 1, 0), kv_tile)),
                  pl.BlockSpec((BLOCK, 2 * A_KV_WIDTH), lambda n: (META_BLK, kv_tile)),
                  pl.BlockSpec((1, A_Q_HEADS, BLOCK, 2 * BLOCK),
                               lambda n: (jnp.clip(n - META_BLK, 0, 2), 0, 0, 0))],
        out_specs=pl.BlockSpec((BLOCK, A_WIDTH), lambda n: (n, 0)),
        out_shape=jax.ShapeDtypeStruct((L_PAD, A_WIDTH), BF16),
        name="swa_mixer",
        compiler_params=_params(1),
    )(sinks.astype(F32), z, z, z, bias)


def _softplus2(s):
    neg_abs = lax.bitcast_convert_type(lax.bitcast_convert_type(s, jnp.uint32) | jnp.uint32(0x80000000), F32)
    return jnp.maximum(s, 0.0) + jnp.log2(1.0 + jnp.exp2(neg_abs))


def _sb_kernel(q_ref, k_ref, v_ref, gate_ref, o_ref, acc_ref, later_ref, dead_ref):
    qi = pl.program_id(1)
    r_iota = lax.broadcasted_iota(jnp.int32, (SB_TK, SB_TK), 0)
    c_iota = lax.broadcasted_iota(jnp.int32, (SB_TK, SB_TK), 1)
    suffix = (r_iota >= c_iota).astype(BF16)

    n_chunks = SB_TQ // SB_TK

    masks = {"tri": c_iota < r_iota, "pad": c_iota >= META_ROW0 - SB_TK,
             "tri_pad": (c_iota < r_iota) & (c_iota >= META_ROW0 - SB_TK), None: None}

    def sweep(tiles, first=False):
        items = [(j, masks[kind], c, hh) for (j, kind, c) in tiles for hh in range(SB_HG)]
        rows = lambda c: slice(c * SB_TK, (c + 1) * SB_TK)
        lanes = lambda hh: slice(hh * B_HEAD_DIM, (hh + 1) * B_HEAD_DIM)
        later = {(c, hh): (jnp.zeros((SB_TK, B_HEAD_DIM), F32) if first else later_ref[hh, rows(c)])
                 for (_, _, c) in tiles for hh in range(SB_HG)}
        acc, s, tw, later_before = {}, {}, {}, {}

        def stage_scores(n):
            j, _, c, hh = items[n]
            k = k_ref[pl.ds(pl.multiple_of(j * SB_TK, SB_TK), SB_TK), lanes(hh)]
            s[n] = lax.dot_general(q_ref[rows(c), lanes(hh)], k, (((1,), (1,)), ((), ())),
                                   preferred_element_type=F32)

        def stage_suffix(n):
            j, vis, c, hh = items[n]
            sp = _softplus2(s[n])
            if vis is not None:
                sp = jnp.where(vis, sp, 0.0)
            tw[n] = jnp.dot(sp.astype(BF16), suffix, preferred_element_type=F32)
            tot = jnp.sum(sp, axis=-1, keepdims=True)
            later_before[n] = later[(c, hh)]
            later[(c, hh)] = later_before[n] + jnp.broadcast_to(tot, later_before[n].shape)

        def stage_values(n):
            j, vis, c, hh = items[n]
            lat = jnp.concatenate([later_before.pop(n)] * (SB_TK // B_HEAD_DIM), axis=1)
            p = jnp.exp2(s.pop(n) - tw.pop(n) - lat)
            if vis is not None:
                p = jnp.where(vis, p, 0.0)
            v = v_ref[pl.ds(pl.multiple_of(j * SB_TK, SB_TK), SB_TK), lanes(hh)]
            pv = jnp.dot(p.astype(BF16), v, preferred_element_type=F32)
            acc[(c, hh)] = acc[(c, hh)] + pv if (c, hh) in acc else pv

        skew = 2
        for step in range(len(items) + 2 * skew):
            if step < len(items):
                stage_scores(step)
            if 0 <= step - skew < len(items):
                stage_suffix(step - skew)
                if step - skew == len(items) - 1:
                    least = functools.reduce(jnp.minimum, later.values())
                    dead_ref[0] = (jnp.min(least) >= SB_DEAD).astype(jnp.int32)
            if 0 <= step - 2 * skew < len(items):
                stage_values(step - 2 * skew)
        for (c, hh), a in acc.items():
            if first:
                acc_ref[hh, rows(c)] = a
            else:
                acc_ref[hh, rows(c)] += a
            later_ref[hh, rows(c)] = later[(c, hh)]

    assert n_chunks == 2
    g0 = qi * n_chunks

    @pl.when(qi == 0)
    def _():
        acc_ref[:, :SB_TK] = jnp.zeros((SB_HG, SB_TK, B_HEAD_DIM), F32)
        sweep([(1, "tri_pad", 1)], first=True)

    @pl.when(qi == 1)
    def _():
        sweep([(3, "tri", 1), (2, "tri", 0), (2, None, 1), (1, "pad", 0)], first=True)

    @pl.when(qi >= 2)
    def _():
        sweep([(g0 + 1, "tri", 1), (g0, "tri", 0), (g0, None, 1), (g0 - 1, None, 0)], first=True)

    def body(t):
        sweep([(g0 + 1 - t, None, 1), (g0 - t, None, 0)])
        return t + 1

    lax.while_loop(lambda t: (g0 - t >= 2) & (dead_ref[0] == 0), body, jnp.int32(2))

    @pl.when((qi == 1) & (dead_ref[0] == 0))
    def _():
        sweep([(1, "pad", 1)])

    @pl.when((qi >= 2) & (dead_ref[0] == 0))
    def _():
        sweep([(2, None, 1), (1, "pad", 0), (1, "pad", 1)])

    for hh in range(SB_HG):
        lanes = slice(hh * B_HEAD_DIM, (hh + 1) * B_HEAD_DIM)
        gate = gate_ref[:, lanes].astype(F32)
        o_ref[:, lanes] = (acc_ref[hh] * (gate * jax.nn.sigmoid(gate))).astype(BF16)


def _stick_breaking(z):
    width = SB_HG * B_HEAD_DIM
    hcols = B_WIDTH // width
    return pl.pallas_call(
        _sb_kernel,
        grid=(B_HEADS // SB_HG, L_PAD // SB_TQ),
        in_specs=[pl.BlockSpec((SB_TQ, width), lambda h, i: (i, h)),
                  pl.BlockSpec((L_PAD, width), lambda h, i: (0, hcols + h)),
                  pl.BlockSpec((L_PAD, width), lambda h, i: (0, 2 * hcols + h)),
                  pl.BlockSpec((SB_TQ, width), lambda h, i: (i, 3 * hcols + h))],
        out_specs=pl.BlockSpec((SB_TQ, width), lambda h, i: (i, h)),
        out_shape=jax.ShapeDtypeStruct((L_PAD, B_WIDTH), BF16),
        scratch_shapes=[pltpu.VMEM((SB_HG, SB_TQ, B_HEAD_DIM), F32),
                        pltpu.VMEM((SB_HG, SB_TQ, B_HEAD_DIM), F32),
                        pltpu.SMEM((1,), jnp.int32)],
        name="stick_breaking_mixer",
        compiler_params=_params(2),
    )(z, z, z, z)


def _q_scale(n_cols, q_cols, scale):
    s = np.ones((1, n_cols), np.float32)
    s[:, :q_cols] = scale
    return jnp.asarray(s)


def _forward_one(x, meta_tokens, rel_bias, w_in_a, sinks_a, w_out_a, w_in_b, w_out_b, ln_g, ln_b):
    x = x.astype(F32)
    head = jnp.concatenate([jnp.zeros((META_ROW0, D_MODEL), F32), meta_tokens.astype(F32)], axis=0)
    bias = _bias_tables(rel_bias)
    scale_a = _q_scale(A_IN, A_WIDTH, A_HEAD_DIM ** -0.5)
    scale_b = _q_scale(B_IN, B_WIDTH, B_HEAD_DIM ** -0.5 * LOG2E)
    h = hb = None
    for i in range(DEPTH):
        j = i // 2
        if i % 2 == 0:
            w_in, scale, tm, tn, w_out = w_in_a[j], scale_a, 768, 1152, w_out_a[j]
        else:
            w_in, scale, tm, tn, w_out = w_in_b[j], scale_b, 1536, 1024, w_out_b[j]
        if i == 0:
            z = _in_proj(x, w_in.astype(F32), scale, FRONT, tn, head=head)
        else:
            z = _in_proj(hb, w_in.astype(F32), scale, tm, tn)
        g = _swa(z, sinks_a[j], bias) if i % 2 == 0 else _stick_breaking(z)
        gamma, beta = ln_g[i][None, :].astype(F32), ln_b[i][None, :].astype(F32)
        if i == 0:
            h, hb = _out_ln(g, w_out.astype(BF16), gamma, beta, head=head, x=x)
        elif i == DEPTH - 1:
            return _out_ln(g, w_out.astype(BF16), gamma, beta, h=h, last=True)
        else:
            h, hb = _out_ln(g, w_out.astype(BF16), gamma, beta, h=h)


def kernel(x, meta_tokens, rel_bias, w_in_a, sinks_a, w_out_a, w_in_b, w_out_b, ln_g, ln_b):
    batch = x.shape[0]
    assert x.shape[1:] == (SEQ, D_MODEL)
    xs = x.reshape(batch * SEQ, D_MODEL)
    outs = [_forward_one(xs[b * SEQ:(b + 1) * SEQ] if batch > 1 else xs, meta_tokens, rel_bias,
                         w_in_a, sinks_a, w_out_a, w_in_b, w_out_b, ln_g, ln_b) for b in range(batch)]
    out = outs[0] if batch == 1 else jnp.concatenate(outs, axis=0)
    return out.reshape(batch, SEQ, D_MODEL).astype(x.dtype)
```

```python
import functools
import math

import jax
import jax.numpy as jnp
import numpy as np
from jax import lax
from jax.experimental import pallas as pl
from jax.experimental.pallas import tpu as pltpu

F32 = jnp.float32
BF16 = jnp.bfloat16

D_MODEL = 2048
SEQ = 16384
DEPTH = 4
N_META = 16
BLOCK = 128
WINDOW = 128
A_HEAD_DIM = 64
A_Q_HEADS = 32
A_KV_HEADS = 4
A_GROUP = 8
A_WIDTH = 2048
A_KV_WIDTH = 256
A_IN = 2 * A_WIDTH + 2 * A_KV_WIDTH
B_HEAD_DIM = 128
B_HEADS = 16
B_WIDTH = 2048
B_IN = 4 * B_WIDTH
N_BUCKETS = 32
MAX_DISTANCE = 128
ALPHA = (2.0 * DEPTH) ** 0.25
LN_EPS = 1e-5
NEG = -1e30
LOG2E = math.log2(math.e)

FRONT = 512
META_ROW0 = FRONT - N_META
L_PAD = FRONT + SEQ
N_BLK = L_PAD // BLOCK
META_BLK = FRONT // BLOCK - 1
META_LOCAL = BLOCK - N_META

VMEM_LIMIT_BYTES = 56 * 1024 * 1024

LN_TM = 512
SB_TQ = 1536
SB_TK = 256
SB_HG = 2
SB_DEAD = 152.0


def _params(n_axes):
    return pltpu.CompilerParams(dimension_semantics=("arbitrary",) * n_axes,
                                vmem_limit_bytes=VMEM_LIMIT_BYTES)


def _cast_weights(w_ref, scale_ref, wb_ref):
    @pl.when(pl.program_id(1) == 0)
    def _():
        wb_ref[...] = (w_ref[...] * scale_ref[...]).astype(BF16)


def _in_proj_kernel(x_ref, w_ref, scale_ref, o_ref, wb_ref):
    _cast_weights(w_ref, scale_ref, wb_ref)
    o_ref[...] = jnp.dot(x_ref[...], wb_ref[...], preferred_element_type=F32).astype(o_ref.dtype)


def _in_proj_first_kernel(head_ref, x_ref, w_ref, scale_ref, o_ref, wb_ref):
    _cast_weights(w_ref, scale_ref, wb_ref)
    x = jnp.where(pl.program_id(1) == 0, head_ref[...], x_ref[...]).astype(BF16)
    o_ref[...] = jnp.dot(x, wb_ref[...], preferred_element_type=F32).astype(o_ref.dtype)


def _in_proj(x, w_all, layer, scale, tm, tn, head=None):
    _, k, n = w_all.shape
    once = pl.Buffered(1)
    w_spec = pl.BlockSpec((None, k, tn), lambda j, i: (layer, 0, j), pipeline_mode=once)
    s_spec = pl.BlockSpec((1, tn), lambda j, i: (0, j))
    if head is None:
        kern = _in_proj_kernel
        in_specs = [pl.BlockSpec((tm, k), lambda j, i: (i, 0)), w_spec, s_spec]
        args = (x, w_all, scale)
    else:
        assert tm == FRONT
        kern = _in_proj_first_kernel
        in_specs = [pl.BlockSpec((tm, k), lambda j, i: (0, 0), pipeline_mode=once),
                    pl.BlockSpec((tm, k), lambda j, i: (jnp.maximum(i - 1, 0), 0)), w_spec, s_spec]
        args = (head, x, w_all, scale)
    return pl.pallas_call(
        kern,
        grid=(n // tn, L_PAD // tm),
        in_specs=in_specs,
        out_specs=pl.BlockSpec((tm, tn), lambda j, i: (i, j)),
        out_shape=jax.ShapeDtypeStruct((L_PAD, n), BF16),
        scratch_shapes=[pltpu.VMEM((k, tn), BF16)],
        name="in_proj",
        compiler_params=_params(2),
    )(*args)


LN_TILES = L_PAD // LN_TM


LN_CHUNK = 128


def _project_and_norm(g_ref, w_ref, h_prev, gamma_ref, beta_ref, y_refs, out_refs):
    i = pl.program_id(0)

    @pl.when(i == 0)
    def _():
        y_refs[1][...] = jnp.zeros(y_refs[1].shape, F32)

    def step(y_new, y_prev):
        y_new[...] = jnp.dot(g_ref[...], w_ref[...], preferred_element_type=F32)
        for c in range(LN_TM // LN_CHUNK):
            rows = slice(c * LN_CHUNK, (c + 1) * LN_CHUNK)
            t = ALPHA * h_prev(rows) + y_prev[rows]
            mu = jnp.mean(t, axis=-1, keepdims=True)
            d = t - mu
            var = jnp.mean(d * d, axis=-1, keepdims=True)
            out = d * lax.rsqrt(var + LN_EPS) * gamma_ref[...] + beta_ref[...]
            for ref in out_refs:
                ref[rows] = out.astype(ref.dtype)

    for parity in range(2):
        pl.when(i % 2 == parity)(functools.partial(step, y_refs[parity], y_refs[1 - parity]))


def _out_ln_first_kernel(g_ref, w_ref, head_ref, x_ref, gamma_ref, beta_ref, hf_ref, hb_ref, ya_ref, yb_ref):
    in_front = pl.program_id(0) - 1 < FRONT // LN_TM
    h_prev = lambda rows: jnp.where(in_front, head_ref[rows], x_ref[rows])
    _project_and_norm(g_ref, w_ref, h_prev, gamma_ref, beta_ref, (ya_ref, yb_ref), (hf_ref, hb_ref))


def _out_ln_mid_kernel(g_ref, w_ref, h_ref, gamma_ref, beta_ref, hf_ref, hb_ref, ya_ref, yb_ref):
    _project_and_norm(g_ref, w_ref, lambda rows: h_ref[rows], gamma_ref, beta_ref, (ya_ref, yb_ref),
                      (hf_ref, hb_ref))


def _out_ln_last_kernel(g_ref, w_ref, h_ref, gamma_ref, beta_ref, o_ref, ya_ref, yb_ref):
    _project_and_norm(g_ref, w_ref, lambda rows: h_ref[rows], gamma_ref, beta_ref, (ya_ref, yb_ref), (o_ref,))


def _out_ln(g, w_all, layer, gamma, beta, *, h=None, head=None, x=None, last=False):
    d = D_MODEL
    assert FRONT == LN_TM
    once = pl.Buffered(1)
    fixed = lambda i: (0, 0)
    prev = lambda i: (jnp.maximum(i - 1, 0), 0)
    prev_tok = lambda i: (jnp.maximum(i - 2, 0), 0)
    tile = lambda imap, **kw: pl.BlockSpec((LN_TM, d), imap, **kw)
    common = [tile(lambda i: (jnp.minimum(i, LN_TILES - 1), 0)),
              pl.BlockSpec((None, d, d), lambda i: (layer, 0, 0), pipeline_mode=once)]
    tail = [pl.BlockSpec((1, d), fixed), pl.BlockSpec((1, d), fixed)]
    both = dict(out_specs=[tile(prev), tile(prev)],
                out_shape=[jax.ShapeDtypeStruct((L_PAD, d), F32), jax.ShapeDtypeStruct((L_PAD, d), BF16)])
    if head is not None:
        kern, mid, args, outs = _out_ln_first_kernel, [tile(fixed, pipeline_mode=once), tile(prev_tok)], (head, x), both
    elif last:
        kern, mid, args = _out_ln_last_kernel, [tile(prev)], (h,)
        outs = dict(out_specs=tile(prev_tok), out_shape=jax.ShapeDtypeStruct((SEQ, d), F32))
    else:
        kern, mid, args, outs = _out_ln_mid_kernel, [tile(prev)], (h,), both
    return pl.pallas_call(
        kern,
        grid=(LN_TILES + 1,),
        in_specs=common + mid + tail,
        scratch_shapes=[pltpu.VMEM((LN_TM, d), F32), pltpu.VMEM((LN_TM, d), F32)],
        name="out_proj_ln",
        compiler_params=_params(1),
        **outs,
    )(g, w_all, *args, gamma, beta)


def _t5_bucket_np(dist):
    max_exact = N_BUCKETS // 2
    d = np.maximum(dist, 0)
    df = np.maximum(d, 1).astype(np.float32)
    large = max_exact + (np.log(df / np.float32(max_exact)) / np.float32(math.log(MAX_DISTANCE / max_exact))
                         * np.float32(N_BUCKETS - max_exact)).astype(np.int32)
    large = np.minimum(large, N_BUCKETS - 1)
    return np.where(d < max_exact, d, large).astype(np.int32)


def _bucket_maps():
    masked = N_BUCKETS
    r = np.arange(BLOCK)[:, None]
    c = np.arange(BLOCK)[None, :]
    out = np.full((3, BLOCK, 2 * BLOCK), masked, np.int32)
    ok = (r >= META_LOCAL) & (c >= META_LOCAL) & (r - c >= 0)
    out[0, :, :BLOCK] = np.where(ok, _t5_bucket_np(r - c), masked)
    own = _t5_bucket_np(r - c)
    prev = _t5_bucket_np(r + BLOCK - c)
    out[1, :, :BLOCK] = np.where(c >= META_LOCAL, _t5_bucket_np(r + BLOCK - c), masked)
    out[1, :, BLOCK:] = np.where(c <= r, own, masked)
    far = _t5_bucket_np(np.array(2 * BLOCK + 1))
    out[2, :, :BLOCK] = np.where(c >= META_LOCAL, far, masked) + 0 * r
    out[2, :, BLOCK:] = np.where(c <= r, own, prev)
    return out


def _bias_kernel(rel_ref, map_ref, o_ref):
    bmap = map_ref[0]

    def one_head(h, carry):
        acc = jnp.full(bmap.shape, NEG, F32)
        for b in range(N_BUCKETS):
            acc = jnp.where(bmap == b, rel_ref[b, h], acc)
        o_ref[0, h] = acc
        return carry

    lax.fori_loop(0, A_Q_HEADS, one_head, 0)


def _bias_tables(rel_bias):
    maps = jnp.asarray(_bucket_maps())
    return pl.pallas_call(
        _bias_kernel,
        grid=(3,),
        in_specs=[pl.BlockSpec(memory_space=pltpu.SMEM),
                  pl.BlockSpec((1, BLOCK, 2 * BLOCK), lambda c: (c, 0, 0))],
        out_specs=pl.BlockSpec((1, A_Q_HEADS, BLOCK, 2 * BLOCK), lambda c: (c, 0, 0, 0)),
        out_shape=jax.ShapeDtypeStruct((3, A_Q_HEADS, BLOCK, 2 * BLOCK), F32),
        name="bias_tables",
        compiler_params=_params(1),
    )(rel_bias.astype(F32), maps)


def _swa_kernel(sink_ref, zc_ref, zp_ref, zm_ref, bias_ref, o_ref):
    kv0 = A_WIDTH
    kcat = jnp.concatenate([zm_ref[:, :A_KV_WIDTH], zp_ref[:, :A_KV_WIDTH],
                            zc_ref[:, kv0:kv0 + A_KV_WIDTH]], axis=0)
    vcat = jnp.concatenate([zm_ref[:, A_KV_WIDTH:], zp_ref[:, A_KV_WIDTH:],
                            zc_ref[:, kv0 + A_KV_WIDTH:kv0 + 2 * A_KV_WIDTH]], axis=0)
    gate0 = kv0 + 2 * A_KV_WIDTH
    lane_k = lax.broadcasted_iota(jnp.int32, (3 * BLOCK, BLOCK), 1) < A_HEAD_DIM
    first = lax.broadcasted_iota(jnp.int32, (BLOCK, BLOCK), 1) < A_HEAD_DIM
    own = (lax.broadcasted_iota(jnp.int32, (BLOCK, BLOCK), 0)
           >= lax.broadcasted_iota(jnp.int32, (BLOCK, BLOCK), 1))
    tiles_per_group = A_GROUP // 2
    n_tiles = A_KV_HEADS * tiles_per_group
    kv_cache = {}

    def kv_of(g):
        if g not in kv_cache:
            pair = slice((g // 2) * BLOCK, (g // 2 + 1) * BLOCK)
            k2 = kcat[:, pair].astype(F32)
            v2 = vcat[:, pair].astype(F32)
            k2r = pltpu.roll(k2, A_HEAD_DIM, axis=1)
            v2r = pltpu.roll(v2, A_HEAD_DIM, axis=1)
            if g % 2 == 0:
                kd, v_lo, v_hi = jnp.where(lane_k, k2, k2r), v2, v2r
            else:
                kd, v_lo, v_hi = jnp.where(lane_k, k2r, k2), v2r, v2
            vd = [jnp.where(lane_k, v_lo, 1.0).astype(BF16), jnp.where(lane_k, 1.0, v_hi).astype(BF16)]
            kv_cache[g] = (kd.astype(BF16), vd)
        return kv_cache[g]

    def scores(tile):
        qt = zc_ref[:, tile * BLOCK:(tile + 1) * BLOCK]
        zero = jnp.zeros_like(qt)
        qq = jnp.concatenate([jnp.where(first, qt, zero), jnp.where(first, zero, qt)], axis=0)
        kd, _ = kv_of(tile // tiles_per_group)
        return lax.dot_general(qq, kd, (((1,), (1,)), ((), ())), preferred_element_type=F32)

    s_next = scores(0)
    for tile in range(n_tiles):
        s = s_next
        if tile + 1 < n_tiles:
            s_next = scores(tile + 1)
        _, vd = kv_of(tile // tiles_per_group)
        cols = slice(tile * BLOCK, (tile + 1) * BLOCK)
        res, sink_term = [], []
        for e in range(2):
            head = 2 * tile + e
            se = s[e * BLOCK:(e + 1) * BLOCK]
            window = jnp.where(own, se[:, 2 * BLOCK:], se[:, BLOCK:2 * BLOCK])
            sc = jnp.concatenate([se[:, :BLOCK], window], axis=1) + bias_ref[0, head]
            sink = sink_ref[head]
            m = jnp.maximum(jnp.max(sc, axis=-1, keepdims=True), sink)
            p = jnp.exp(sc - m)
            pw = p[:, BLOCK:]
            pf = jnp.concatenate([p[:, :BLOCK], jnp.where(own, 0.0, pw), jnp.where(own, pw, 0.0)],
                                 axis=1).astype(BF16)
            res.append(jnp.dot(pf, vd[e], preferred_element_type=F32))
            sink_term.append(jnp.exp(sink - m))
        num = jnp.where(first, res[0], res[1])
        den = (pltpu.roll(jnp.where(first, res[1], res[0]), A_HEAD_DIM, axis=1)
               + jnp.where(first, sink_term[0], sink_term[1]))
        gate = zc_ref[:, gate0 + tile * BLOCK:gate0 + (tile + 1) * BLOCK].astype(F32)
        o_ref[:, cols] = (num / den * (gate * jax.nn.sigmoid(gate))).astype(BF16)


def _swa(z, sinks, bias):
    kv_tile = A_WIDTH // (2 * A_KV_WIDTH)
    return pl.pallas_call(
        _swa_kernel,
        grid=(N_BLK,),
        in_specs=[pl.BlockSpec(memory_space=pltpu.SMEM),
                  pl.BlockSpec((BLOCK, A_IN), lambda n: (n, 0)),
                  pl.BlockSpec((BLOCK, 2 * A_KV_WIDTH), lambda n: (jnp.maximum(n - 1, 0), kv_tile)),
                  pl.BlockSpec((BLOCK, 2 * A_KV_WIDTH), lambda n: (META_BLK, kv_tile)),
                  pl.BlockSpec((1, A_Q_HEADS, BLOCK, 2 * BLOCK),
                               lambda n: (jnp.clip(n - META_BLK, 0, 2), 0, 0, 0))],
        out_specs=pl.BlockSpec((BLOCK, A_WIDTH), lambda n: (n, 0)),
        out_shape=jax.ShapeDtypeStruct((L_PAD, A_WIDTH), BF16),
        name="swa_mixer",
        compiler_params=_params(1),
    )(sinks.astype(F32), z, z, z, bias)


def _softplus2(s):
    neg_abs = lax.bitcast_convert_type(lax.bitcast_convert_type(s, jnp.uint32) | jnp.uint32(0x80000000), F32)
    return jnp.maximum(s, 0.0) + jnp.log2(1.0 + jnp.exp2(neg_abs))


def _sb_kernel(q_ref, k_ref, v_ref, gate_ref, o_ref, acc_ref, later_ref, dead_ref):
    qi = pl.program_id(1)
    r_iota = lax.broadcasted_iota(jnp.int32, (SB_TK, SB_TK), 0)
    c_iota = lax.broadcasted_iota(jnp.int32, (SB_TK, SB_TK), 1)
    suffix = (r_iota >= c_iota).astype(BF16)

    n_chunks = SB_TQ // SB_TK

    masks = {"tri": c_iota < r_iota, "pad": c_iota >= META_ROW0 - SB_TK,
             "tri_pad": (c_iota < r_iota) & (c_iota >= META_ROW0 - SB_TK), None: None}

    def sweep(tiles, first=False):
        items = [(j, masks[kind], c, hh) for (j, kind, c) in tiles for hh in range(SB_HG)]
        rows = lambda c: slice(c * SB_TK, (c + 1) * SB_TK)
        lanes = lambda hh: slice(hh * B_HEAD_DIM, (hh + 1) * B_HEAD_DIM)
        later = {(c, hh): (jnp.zeros((SB_TK, B_HEAD_DIM), F32) if first else later_ref[hh, rows(c)])
                 for (_, _, c) in tiles for hh in range(SB_HG)}
        acc, s, tw, later_before = {}, {}, {}, {}

        def stage_scores(n):
            j, _, c, hh = items[n]
            k = k_ref[pl.ds(pl.multiple_of(j * SB_TK, SB_TK), SB_TK), lanes(hh)]
            s[n] = lax.dot_general(q_ref[rows(c), lanes(hh)], k, (((1,), (1,)), ((), ())),
                                   preferred_element_type=F32)

        def stage_suffix(n):
            j, vis, c, hh = items[n]
            sp = _softplus2(s[n])
            if vis is not None:
                sp = jnp.where(vis, sp, 0.0)
            tw[n] = jnp.dot(sp.astype(BF16), suffix, preferred_element_type=F32)
            tot = jnp.sum(sp, axis=-1, keepdims=True)
            later_before[n] = later[(c, hh)]
            later[(c, hh)] = later_before[n] + jnp.broadcast_to(tot, later_before[n].shape)

        def stage_values(n):
            j, vis, c, hh = items[n]
            lat = jnp.concatenate([later_before.pop(n)] * (SB_TK // B_HEAD_DIM), axis=1)
            p = jnp.exp2(s.pop(n) - tw.pop(n) - lat)
            if vis is not None:
                p = jnp.where(vis, p, 0.0)
            v = v_ref[pl.ds(pl.multiple_of(j * SB_TK, SB_TK), SB_TK), lanes(hh)]
            pv = jnp.dot(p.astype(BF16), v, preferred_element_type=F32)
            acc[(c, hh)] = acc[(c, hh)] + pv if (c, hh) in acc else pv

        skew = 2
        for step in range(len(items) + 2 * skew):
            if step < len(items):
                stage_scores(step)
            if 0 <= step - skew < len(items):
                stage_suffix(step - skew)
                if step - skew == len(items) - 1:
                    least = functools.reduce(jnp.minimum, later.values())
                    dead_ref[0] = (jnp.min(least) >= SB_DEAD).astype(jnp.int32)
            if 0 <= step - 2 * skew < len(items):
                stage_values(step - 2 * skew)
        for (c, hh), a in acc.items():
            if first:
                acc_ref[hh, rows(c)] = a
            else:
                acc_ref[hh, rows(c)] += a
            later_ref[hh, rows(c)] = later[(c, hh)]

    assert n_chunks >= 3
    g0 = qi * n_chunks
    chunks = range(n_chunks)
    to_boundary = lambda c: [(j, None, c) for j in range(c + 1, 1, -1)] + [(1, "pad", c)]

    @pl.when(qi == 0)
    def _():
        acc_ref[:, :SB_TK] = jnp.zeros((SB_HG, SB_TK, B_HEAD_DIM), F32)
        tiles = [(1, "tri_pad", 1)]
        for c in chunks[2:]:
            tiles += [(c, "tri", c)] + [(j, None, c) for j in range(c - 1, 1, -1)] + [(1, "pad", c)]
        sweep(tiles, first=True)

    @pl.when(qi >= 1)
    def _():
        sweep([(g0 + c, "tri", c) for c in chunks] + [(g0 + c - 1, None, c) for c in chunks], first=True)

    def body(t):
        sweep([(g0 + c - t, None, c) for c in chunks])
        return t + 1

    lax.while_loop(lambda t: (g0 - t >= 2) & (dead_ref[0] == 0), body, jnp.int32(2))

    @pl.when((qi >= 1) & (dead_ref[0] == 0))
    def _():
        sweep([tile for c in chunks for tile in to_boundary(c)])

    for hh in range(SB_HG):
        lanes = slice(hh * B_HEAD_DIM, (hh + 1) * B_HEAD_DIM)
        gate = gate_ref[:, lanes].astype(F32)
        o_ref[:, lanes] = (acc_ref[hh] * (gate * jax.nn.sigmoid(gate))).astype(BF16)


def _stick_breaking(z):
    width = SB_HG * B_HEAD_DIM
    hcols = B_WIDTH // width
    return pl.pallas_call(
        _sb_kernel,
        grid=(B_HEADS // SB_HG, L_PAD // SB_TQ),
        in_specs=[pl.BlockSpec((SB_TQ, width), lambda h, i: (i, h)),
                  pl.BlockSpec((L_PAD, width), lambda h, i: (0, hcols + h)),
                  pl.BlockSpec((L_PAD, width), lambda h, i: (0, 2 * hcols + h)),
                  pl.BlockSpec((SB_TQ, width), lambda h, i: (i, 3 * hcols + h))],
        out_specs=pl.BlockSpec((SB_TQ, width), lambda h, i: (i, h)),
        out_shape=jax.ShapeDtypeStruct((L_PAD, B_WIDTH), BF16),
        scratch_shapes=[pltpu.VMEM((SB_HG, SB_TQ, B_HEAD_DIM), F32),
                        pltpu.VMEM((SB_HG, SB_TQ, B_HEAD_DIM), F32),
                        pltpu.SMEM((1,), jnp.int32)],
        name="stick_breaking_mixer",
        compiler_params=_params(2),
    )(z, z, z, z)


def _q_scale(n_cols, q_cols, scale):
    s = np.ones((1, n_cols), np.float32)
    s[:, :q_cols] = scale
    return jnp.asarray(s)


def _forward_one(x, meta_tokens, rel_bias, w_in_a, sinks_a, w_out_a, w_in_b, w_out_b, ln_g, ln_b):
    x = x.astype(F32)
    head = jnp.concatenate([jnp.zeros((META_ROW0, D_MODEL), F32), meta_tokens.astype(F32)], axis=0)
    bias = _bias_tables(rel_bias)
    scale_a = _q_scale(A_IN, A_WIDTH, A_HEAD_DIM ** -0.5)
    scale_b = _q_scale(B_IN, B_WIDTH, B_HEAD_DIM ** -0.5 * LOG2E)
    w_in = (w_in_a.astype(F32), w_in_b.astype(F32))
    w_out = (w_out_a.astype(BF16), w_out_b.astype(BF16))
    in_scale = (scale_a, scale_b)
    in_tn = (1536, 1024)
    h = hb = None
    for i in range(DEPTH):
        j, mixer = i // 2, i % 2
        if i == 0:
            z = _in_proj(x, w_in[mixer], j, in_scale[mixer], FRONT, in_tn[mixer], head=head)
        else:
            z = _in_proj(hb, w_in[mixer], j, in_scale[mixer], 1536, in_tn[mixer])
        g = _swa(z, sinks_a[j], bias) if mixer == 0 else _stick_breaking(z)
        gamma, beta = ln_g[i][None, :].astype(F32), ln_b[i][None, :].astype(F32)
        if i == 0:
            h, hb = _out_ln(g, w_out[mixer], j, gamma, beta, head=head, x=x)
        elif i == DEPTH - 1:
            return _out_ln(g, w_out[mixer], j, gamma, beta, h=h, last=True)
        else:
            h, hb = _out_ln(g, w_out[mixer], j, gamma, beta, h=h)


def kernel(x, meta_tokens, rel_bias, w_in_a, sinks_a, w_out_a, w_in_b, w_out_b, ln_g, ln_b):
    batch = x.shape[0]
    assert x.shape[1:] == (SEQ, D_MODEL)
    xs = x.reshape(batch * SEQ, D_MODEL)
    outs = [_forward_one(xs[b * SEQ:(b + 1) * SEQ] if batch > 1 else xs, meta_tokens, rel_bias,
                         w_in_a, sinks_a, w_out_a, w_in_b, w_out_b, ln_g, ln_b) for b in range(batch)]
    out = outs[0] if batch == 1 else jnp.concatenate(outs, axis=0)
    return out.reshape(batch, SEQ, D_MODEL).astype(x.dtype)
```

```python
import functools
import math

import jax
import jax.numpy as jnp
import numpy as np
from jax import lax
from jax.experimental import pallas as pl
from jax.experimental.pallas import tpu as pltpu

F32 = jnp.float32
BF16 = jnp.bfloat16

D_MODEL = 2048
SEQ = 16384
DEPTH = 4
N_META = 16
BLOCK = 128
WINDOW = 128
A_HEAD_DIM = 64
A_Q_HEADS = 32
A_KV_HEADS = 4
A_GROUP = 8
A_WIDTH = 2048
A_KV_WIDTH = 256
A_IN = 2 * A_WIDTH + 2 * A_KV_WIDTH
B_HEAD_DIM = 128
B_HEADS = 16
B_WIDTH = 2048
B_IN = 4 * B_WIDTH
N_BUCKETS = 32
MAX_DISTANCE = 128
ALPHA = (2.0 * DEPTH) ** 0.25
LN_EPS = 1e-5
NEG = -1e30
LOG2E = math.log2(math.e)

FRONT = 512
META_ROW0 = FRONT - N_META
L_PAD = FRONT + SEQ
N_BLK = L_PAD // BLOCK
META_BLK = FRONT // BLOCK - 1
META_LOCAL = BLOCK - N_META

VMEM_LIMIT_BYTES = 56 * 1024 * 1024
VMEM_PLAN_FRACTION = 0.9

LN_TM = 512
SB_TQ = 1536
SB_TK = 256
SB_HG = 2
SB_DEAD = 152.0


def _params(n_axes):
    return pltpu.CompilerParams(dimension_semantics=("arbitrary",) * n_axes,
                                vmem_limit_bytes=VMEM_LIMIT_BYTES)


def _cast_weights(w_ref, scale_ref, wb_ref):
    @pl.when(pl.program_id(1) == 0)
    def _():
        wb_ref[...] = (w_ref[...] * scale_ref[...]).astype(BF16)


def _in_proj_kernel(x_ref, w_ref, scale_ref, o_ref, wb_ref):
    _cast_weights(w_ref, scale_ref, wb_ref)
    o_ref[...] = jnp.dot(x_ref[...], wb_ref[...], preferred_element_type=F32).astype(o_ref.dtype)


def _in_proj_first_kernel(head_ref, x_ref, w_ref, scale_ref, o_ref, wb_ref):
    _cast_weights(w_ref, scale_ref, wb_ref)
    x = jnp.where(pl.program_id(1) == 0, head_ref[...], x_ref[...]).astype(BF16)
    o_ref[...] = jnp.dot(x, wb_ref[...], preferred_element_type=F32).astype(o_ref.dtype)


def _in_proj(x, w_all, layer, scale, tm, tn, head=None):
    _, k, n = w_all.shape
    once = pl.Buffered(1)
    planned = (2 * k * tn * 4 + k * tn * 2 + 2 * tm * k * x.dtype.itemsize + 2 * tm * tn * 2 + tm * tn * 4
               + (0 if head is None else tm * k * (4 + 2)))
    w_mode = pl.Buffered(2 if planned <= VMEM_PLAN_FRACTION * VMEM_LIMIT_BYTES else 1)
    w_spec = pl.BlockSpec((None, k, tn), lambda j, i: (layer, 0, j), pipeline_mode=w_mode)
    s_spec = pl.BlockSpec((1, tn), lambda j, i: (0, j))
    if head is None:
        kern = _in_proj_kernel
        in_specs = [pl.BlockSpec((tm, k), lambda j, i: (i, 0)), w_spec, s_spec]
        args = (x, w_all, scale)
    else:
        assert tm == FRONT
        kern = _in_proj_first_kernel
        in_specs = [pl.BlockSpec((tm, k), lambda j, i: (0, 0), pipeline_mode=once),
                    pl.BlockSpec((tm, k), lambda j, i: (jnp.maximum(i - 1, 0), 0)), w_spec, s_spec]
        args = (head, x, w_all, scale)
    return pl.pallas_call(
        kern,
        grid=(n // tn, L_PAD // tm),
        in_specs=in_specs,
        out_specs=pl.BlockSpec((tm, tn), lambda j, i: (i, j)),
        out_shape=jax.ShapeDtypeStruct((L_PAD, n), BF16),
        scratch_shapes=[pltpu.VMEM((k, tn), BF16)],
        name="in_proj",
        compiler_params=_params(2),
    )(*args)


LN_TILES = L_PAD // LN_TM


LN_CHUNK = 128


def _project_and_norm(g_ref, w_ref, h_prev, gamma_ref, beta_ref, y_refs, out_refs):
    i = pl.program_id(0)

    @pl.when(i == 0)
    def _():
        y_refs[1][...] = jnp.zeros(y_refs[1].shape, F32)

    def step(y_new, y_prev):
        y_new[...] = jnp.dot(g_ref[...], w_ref[...], preferred_element_type=F32)
        for c in range(LN_TM // LN_CHUNK):
            rows = slice(c * LN_CHUNK, (c + 1) * LN_CHUNK)
            t = ALPHA * h_prev(rows) + y_prev[rows]
            mu = jnp.mean(t, axis=-1, keepdims=True)
            d = t - mu
            var = jnp.mean(d * d, axis=-1, keepdims=True)
            out = d * lax.rsqrt(var + LN_EPS) * gamma_ref[...] + beta_ref[...]
            for ref in out_refs:
                ref[rows] = out.astype(ref.dtype)

    for parity in range(2):
        pl.when(i % 2 == parity)(functools.partial(step, y_refs[parity], y_refs[1 - parity]))


def _out_ln_mid_kernel(g_ref, w_ref, h_ref, gamma_ref, beta_ref, hf_ref, hb_ref, ya_ref, yb_ref):
    _project_and_norm(g_ref, w_ref, lambda rows: h_ref[rows], gamma_ref, beta_ref, (ya_ref, yb_ref),
                      (hf_ref, hb_ref))


def _out_ln_last_kernel(g_ref, w_ref, h_ref, gamma_ref, beta_ref, o_ref, ya_ref, yb_ref):
    _project_and_norm(g_ref, w_ref, lambda rows: h_ref[rows], gamma_ref, beta_ref, (ya_ref, yb_ref), (o_ref,))


def _out_ln(g, w_all, layer, gamma, beta, h, last=False):
    d = D_MODEL
    assert FRONT == LN_TM
    fixed = lambda i: (0, 0)
    prev = lambda i: (jnp.maximum(i - 1, 0), 0)
    prev_tok = lambda i: (jnp.maximum(i - 2, 0), 0)
    tile = lambda imap: pl.BlockSpec((LN_TM, d), imap)
    if last:
        kern = _out_ln_last_kernel
        outs = dict(out_specs=tile(prev_tok), out_shape=jax.ShapeDtypeStruct((SEQ, d), F32))
    else:
        kern = _out_ln_mid_kernel
        outs = dict(out_specs=[tile(prev), tile(prev)],
                    out_shape=[jax.ShapeDtypeStruct((L_PAD, d), F32), jax.ShapeDtypeStruct((L_PAD, d), BF16)])
    return pl.pallas_call(
        kern,
        grid=(LN_TILES + 1,),
        in_specs=[tile(lambda i: (jnp.minimum(i, LN_TILES - 1), 0)),
                  pl.BlockSpec((None, d, d), lambda i: (layer, 0, 0), pipeline_mode=pl.Buffered(1)),
                  tile(prev), pl.BlockSpec((1, d), fixed), pl.BlockSpec((1, d), fixed)],
        scratch_shapes=[pltpu.VMEM((LN_TM, d), F32), pltpu.VMEM((LN_TM, d), F32)],
        name="out_proj_ln",
        compiler_params=_params(1),
        **outs,
    )(g, w_all, h, gamma, beta)


def _t5_bucket_np(dist):
    max_exact = N_BUCKETS // 2
    d = np.maximum(dist, 0)
    df = np.maximum(d, 1).astype(np.float32)
    large = max_exact + (np.log(df / np.float32(max_exact)) / np.float32(math.log(MAX_DISTANCE / max_exact))
                         * np.float32(N_BUCKETS - max_exact)).astype(np.int32)
    large = np.minimum(large, N_BUCKETS - 1)
    return np.where(d < max_exact, d, large).astype(np.int32)


def _bucket_maps():
    masked = N_BUCKETS
    r = np.arange(BLOCK)[:, None]
    c = np.arange(BLOCK)[None, :]
    out = np.full((3, BLOCK, 2 * BLOCK), masked, np.int32)
    ok = (r >= META_LOCAL) & (c >= META_LOCAL) & (r - c >= 0)
    out[0, :, :BLOCK] = np.where(ok, _t5_bucket_np(r - c), masked)
    own = _t5_bucket_np(r - c)
    prev = _t5_bucket_np(r + BLOCK - c)
    out[1, :, :BLOCK] = np.where(c >= META_LOCAL, _t5_bucket_np(r + BLOCK - c), masked)
    out[1, :, BLOCK:] = np.where(c <= r, own, masked)
    far = _t5_bucket_np(np.array(2 * BLOCK + 1))
    out[2, :, :BLOCK] = np.where(c >= META_LOCAL, far, masked) + 0 * r
    out[2, :, BLOCK:] = np.where(c <= r, own, prev)
    return out


def _bias_kernel(rel_ref, map_ref, o_ref):
    bmap = map_ref[0]

    def one_head(h, carry):
        acc = jnp.full(bmap.shape, NEG, F32)
        for b in range(N_BUCKETS):
            acc = jnp.where(bmap == b, rel_ref[b, h] * LOG2E, acc)
        o_ref[0, h] = acc
        return carry

    lax.fori_loop(0, A_Q_HEADS, one_head, 0)


def _bias_tables(rel_bias):
    maps = jnp.asarray(_bucket_maps())
    return pl.pallas_call(
        _bias_kernel,
        grid=(3,),
        in_specs=[pl.BlockSpec(memory_space=pltpu.SMEM),
                  pl.BlockSpec((1, BLOCK, 2 * BLOCK), lambda c: (c, 0, 0))],
        out_specs=pl.BlockSpec((1, A_Q_HEADS, BLOCK, 2 * BLOCK), lambda c: (c, 0, 0, 0)),
        out_shape=jax.ShapeDtypeStruct((3, A_Q_HEADS, BLOCK, 2 * BLOCK), F32),
        name="bias_tables",
        compiler_params=_params(1),
    )(rel_bias.astype(F32), maps)


SWA_TILES = A_Q_HEADS // 2
SWA_AHEAD = 2


def _swa_block(sink_ref, zc_ref, zp_ref, zm_ref, bias_ref, o_ref, between=None):
    kv0 = A_WIDTH
    kcat = jnp.concatenate([zm_ref[:, :A_KV_WIDTH], zp_ref[:, :A_KV_WIDTH],
                            zc_ref[:, kv0:kv0 + A_KV_WIDTH]], axis=0)
    vcat = jnp.concatenate([zm_ref[:, A_KV_WIDTH:], zp_ref[:, A_KV_WIDTH:],
                            zc_ref[:, kv0 + A_KV_WIDTH:kv0 + 2 * A_KV_WIDTH]], axis=0)
    gate0 = kv0 + 2 * A_KV_WIDTH
    lane_k = lax.broadcasted_iota(jnp.int32, (3 * BLOCK, BLOCK), 1) < A_HEAD_DIM
    first = lax.broadcasted_iota(jnp.int32, (BLOCK, BLOCK), 1) < A_HEAD_DIM
    own = (lax.broadcasted_iota(jnp.int32, (BLOCK, BLOCK), 0)
           >= lax.broadcasted_iota(jnp.int32, (BLOCK, BLOCK), 1))
    tiles_per_group = A_GROUP // 2
    n_tiles = SWA_TILES
    kv_cache = {}

    def kv_of(g):
        if g not in kv_cache:
            pair = slice((g // 2) * BLOCK, (g // 2 + 1) * BLOCK)
            k2 = kcat[:, pair].astype(F32)
            v2 = vcat[:, pair].astype(F32)
            k2r = pltpu.roll(k2, A_HEAD_DIM, axis=1)
            v2r = pltpu.roll(v2, A_HEAD_DIM, axis=1)
            if g % 2 == 0:
                kd, v_lo, v_hi = jnp.where(lane_k, k2, k2r), v2, v2r
            else:
                kd, v_lo, v_hi = jnp.where(lane_k, k2r, k2), v2r, v2
            vd = [jnp.where(lane_k, v_lo, 1.0).astype(BF16), jnp.where(lane_k, 1.0, v_hi).astype(BF16)]
            kv_cache[g] = (kd.astype(BF16), vd)
        return kv_cache[g]

    def scores(tile):
        qt = zc_ref[:, tile * BLOCK:(tile + 1) * BLOCK]
        zero = jnp.zeros_like(qt)
        qq = jnp.concatenate([jnp.where(first, qt, zero), jnp.where(first, zero, qt)], axis=0)
        kd, _ = kv_of(tile // tiles_per_group)
        return lax.dot_general(qq, kd, (((1,), (1,)), ((), ())), preferred_element_type=F32)

    pending = [scores(t) for t in range(SWA_AHEAD)]
    for tile in range(n_tiles):
        s = pending.pop(0)
        if tile + SWA_AHEAD < n_tiles:
            pending.append(scores(tile + SWA_AHEAD))
        if between is not None:
            between(tile)
        _, vd = kv_of(tile // tiles_per_group)
        cols = slice(tile * BLOCK, (tile + 1) * BLOCK)
        res, sink_term = [], []
        for e in range(2):
            head = 2 * tile + e
            se = s[e * BLOCK:(e + 1) * BLOCK]
            window = jnp.where(own, se[:, 2 * BLOCK:], se[:, BLOCK:2 * BLOCK])
            sc = jnp.concatenate([se[:, :BLOCK], window], axis=1) + bias_ref[0, head]
            sink = sink_ref[head] * LOG2E
            m = jnp.maximum(jnp.max(sc, axis=-1, keepdims=True), sink)
            p = jnp.exp2(sc - m)
            pw = p[:, BLOCK:]
            pf = jnp.concatenate([p[:, :BLOCK], jnp.where(own, 0.0, pw), jnp.where(own, pw, 0.0)],
                                 axis=1).astype(BF16)
            res.append(jnp.dot(pf, vd[e], preferred_element_type=F32))
            sink_term.append(jnp.exp2(sink - m))
        num = jnp.where(first, res[0], res[1])
        den = (pltpu.roll(jnp.where(first, res[1], res[0]), A_HEAD_DIM, axis=1)
               + jnp.where(first, sink_term[0], sink_term[1]))
        gate = zc_ref[:, gate0 + tile * BLOCK:gate0 + (tile + 1) * BLOCK].astype(F32)
        o_ref[:, cols] = (num / den * (gate * jax.nn.sigmoid(gate))).astype(BF16)


OUT_SLABS = 4
NORM_ROWS = BLOCK // (SWA_TILES - 2 * OUT_SLABS)


def _swa_deepnorm(n, attn_refs, w_ref, h_prev, gamma_ref, beta_ref, hf_ref, hb_ref, g_refs, y_ref):
    @pl.when(n == 0)
    def _():
        g_refs[1][...] = jnp.zeros(g_refs[1].shape, BF16)

    def step(g_new, g_prev):
        slab = D_MODEL // OUT_SLABS

        def between(tile):
            if tile % 2 == 0 and tile < 2 * OUT_SLABS:
                cols = slice((tile // 2) * slab, (tile // 2 + 1) * slab)
                y_ref[:, cols] = jnp.dot(g_prev[...], w_ref[:, cols], preferred_element_type=F32)
            elif tile >= 2 * OUT_SLABS:
                r = tile - 2 * OUT_SLABS
                rows = slice(r * NORM_ROWS, (r + 1) * NORM_ROWS)
                t = ALPHA * h_prev(rows) + y_ref[rows]
                mu = jnp.mean(t, axis=-1, keepdims=True)
                d = t - mu
                var = jnp.mean(d * d, axis=-1, keepdims=True)
                out = d * lax.rsqrt(var + LN_EPS) * gamma_ref[...] + beta_ref[...]
                hf_ref[rows] = out
                hb_ref[rows] = out.astype(BF16)

        _swa_block(*attn_refs, g_new, between)

    for parity in range(2):
        pl.when(n % 2 == parity)(functools.partial(step, g_refs[parity], g_refs[1 - parity]))


def _swa_deepnorm_first_kernel(sink_ref, zc_ref, zp_ref, zm_ref, bias_ref, w_ref, head_ref, x_ref,
                               gamma_ref, beta_ref, hf_ref, hb_ref, ga_ref, gb_ref, y_ref):
    n = pl.program_id(0)
    in_front = n - 1 < FRONT // BLOCK
    h_prev = lambda rows: jnp.where(in_front, head_ref[rows], x_ref[rows])
    _swa_deepnorm(n, (sink_ref, zc_ref, zp_ref, zm_ref, bias_ref), w_ref, h_prev, gamma_ref, beta_ref,
                  hf_ref, hb_ref, (ga_ref, gb_ref), y_ref)


def _swa_deepnorm_mid_kernel(sink_ref, zc_ref, zp_ref, zm_ref, bias_ref, w_ref, h_ref,
                             gamma_ref, beta_ref, hf_ref, hb_ref, ga_ref, gb_ref, y_ref):
    _swa_deepnorm(pl.program_id(0), (sink_ref, zc_ref, zp_ref, zm_ref, bias_ref), w_ref,
                  lambda rows: h_ref[rows], gamma_ref, beta_ref, hf_ref, hb_ref, (ga_ref, gb_ref), y_ref)


def _swa_out_ln(z, sinks, bias, w_all, layer, gamma, beta, *, h=None, head=None, x=None):
    d = D_MODEL
    kv_tile = A_WIDTH // (2 * A_KV_WIDTH)
    front_blocks = FRONT // BLOCK
    once = pl.Buffered(1)
    cur = lambda n: jnp.minimum(n, N_BLK - 1)
    prev = lambda n: (jnp.maximum(n - 1, 0), 0)
    fixed = lambda n: (0, 0)
    rows = lambda imap: pl.BlockSpec((BLOCK, d), imap)
    attn = [pl.BlockSpec(memory_space=pltpu.SMEM),
            pl.BlockSpec((BLOCK, A_IN), lambda n: (cur(n), 0)),
            pl.BlockSpec((BLOCK, 2 * A_KV_WIDTH), lambda n: (jnp.maximum(cur(n) - 1, 0), kv_tile)),
            pl.BlockSpec((BLOCK, 2 * A_KV_WIDTH), lambda n: (META_BLK, kv_tile)),
            pl.BlockSpec((1, A_Q_HEADS, BLOCK, 2 * BLOCK),
                         lambda n: (jnp.clip(cur(n) - META_BLK, 0, 2), 0, 0, 0)),
            pl.BlockSpec((None, d, d), lambda n: (layer, 0, 0), pipeline_mode=once)]
    if head is not None:
        kern, args = _swa_deepnorm_first_kernel, (head, x)
        resid = [rows(lambda n: (jnp.clip(n - 1, 0, front_blocks - 1), 0)),
                 rows(lambda n: (jnp.maximum(n - 1 - front_blocks, 0), 0))]
    else:
        kern, args, resid = _swa_deepnorm_mid_kernel, (h,), [rows(prev)]
    return pl.pallas_call(
        kern,
        grid=(N_BLK + 1,),
        in_specs=attn + resid + [pl.BlockSpec((1, d), fixed), pl.BlockSpec((1, d), fixed)],
        out_specs=[rows(prev), rows(prev)],
        out_shape=[jax.ShapeDtypeStruct((L_PAD, d), F32), jax.ShapeDtypeStruct((L_PAD, d), BF16)],
        scratch_shapes=[pltpu.VMEM((BLOCK, A_WIDTH), BF16), pltpu.VMEM((BLOCK, A_WIDTH), BF16),
                        pltpu.VMEM((BLOCK, d), F32)],
        name="swa_mixer_out_proj_ln",
        compiler_params=_params(1),
    )(sinks.astype(F32), z, z, z, bias, w_all, *args, gamma, beta)


def _softplus2(s):
    neg_abs = lax.bitcast_convert_type(lax.bitcast_convert_type(s, jnp.uint32) | jnp.uint32(0x80000000), F32)
    return jnp.maximum(s, 0.0) + jnp.log2(1.0 + jnp.exp2(neg_abs))


def _sb_kernel(q_ref, k_ref, v_ref, gate_ref, o_ref, acc_ref, later_ref, dead_ref):
    qi = pl.program_id(1)
    r_iota = lax.broadcasted_iota(jnp.int32, (SB_TK, SB_TK), 0)
    c_iota = lax.broadcasted_iota(jnp.int32, (SB_TK, SB_TK), 1)
    suffix = (r_iota >= c_iota).astype(BF16)

    n_chunks = SB_TQ // SB_TK

    masks = {"tri": c_iota < r_iota, "pad": c_iota >= META_ROW0 - SB_TK,
             "tri_pad": (c_iota < r_iota) & (c_iota >= META_ROW0 - SB_TK), None: None}

    def sweep(tiles, first=False):
        items = [(j, masks[kind], c, hh) for (j, kind, c) in tiles for hh in range(SB_HG)]
        rows = lambda c: slice(c * SB_TK, (c + 1) * SB_TK)
        lanes = lambda hh: slice(hh * B_HEAD_DIM, (hh + 1) * B_HEAD_DIM)
        later = {(c, hh): (jnp.zeros((SB_TK, B_HEAD_DIM), F32) if first else later_ref[hh, rows(c)])
                 for (_, _, c) in tiles for hh in range(SB_HG)}
        acc, s, tw, later_before = {}, {}, {}, {}

        def stage_scores(n):
            j, _, c, hh = items[n]
            k = k_ref[pl.ds(pl.multiple_of(j * SB_TK, SB_TK), SB_TK), lanes(hh)]
            s[n] = lax.dot_general(q_ref[rows(c), lanes(hh)], k, (((1,), (1,)), ((), ())),
                                   preferred_element_type=F32)

        def stage_suffix(n):
            j, vis, c, hh = items[n]
            sp = _softplus2(s[n])
            if vis is not None:
                sp = jnp.where(vis, sp, 0.0)
            tw[n] = jnp.dot(sp.astype(BF16), suffix, preferred_element_type=F32)
            tot = jnp.sum(sp, axis=-1, keepdims=True)
            later_before[n] = later[(c, hh)]
            later[(c, hh)] = later_before[n] + jnp.broadcast_to(tot, later_before[n].shape)

        def stage_values(n):
            j, vis, c, hh = items[n]
            lat = jnp.concatenate([later_before.pop(n)] * (SB_TK // B_HEAD_DIM), axis=1)
            p = jnp.exp2(s.pop(n) - tw.pop(n) - lat)
            if vis is not None:
                p = jnp.where(vis, p, 0.0)
            v = v_ref[pl.ds(pl.multiple_of(j * SB_TK, SB_TK), SB_TK), lanes(hh)]
            pv = jnp.dot(p.astype(BF16), v, preferred_element_type=F32)
            acc[(c, hh)] = acc[(c, hh)] + pv if (c, hh) in acc else pv

        skew = 2
        for step in range(len(items) + 2 * skew):
            if step < len(items):
                stage_scores(step)
            if 0 <= step - skew < len(items):
                stage_suffix(step - skew)
                if step - skew == len(items) - 1:
                    least = functools.reduce(jnp.minimum, later.values())
                    dead_ref[0] = (jnp.min(least) >= SB_DEAD).astype(jnp.int32)
            if 0 <= step - 2 * skew < len(items):
                stage_values(step - 2 * skew)
        for (c, hh), a in acc.items():
            if first:
                acc_ref[hh, rows(c)] = a
            else:
                acc_ref[hh, rows(c)] += a
            later_ref[hh, rows(c)] = later[(c, hh)]

    assert n_chunks >= 3
    g0 = qi * n_chunks
    chunks = range(n_chunks)
    to_boundary = lambda c: [(j, None, c) for j in range(c + 1, 1, -1)] + [(1, "pad", c)]

    @pl.when(qi == 0)
    def _():
        acc_ref[:, :SB_TK] = jnp.zeros((SB_HG, SB_TK, B_HEAD_DIM), F32)
        tiles = [(1, "tri_pad", 1)]
        for c in chunks[2:]:
            tiles += [(c, "tri", c)] + [(j, None, c) for j in range(c - 1, 1, -1)] + [(1, "pad", c)]
        sweep(tiles, first=True)

    @pl.when(qi >= 1)
    def _():
        sweep([(g0 + c, "tri", c) for c in chunks] + [(g0 + c - 1, None, c) for c in chunks], first=True)

    def body(t):
        sweep([(g0 + c - t, None, c) for c in chunks])
        return t + 1

    lax.while_loop(lambda t: (g0 - t >= 2) & (dead_ref[0] == 0), body, jnp.int32(2))

    @pl.when((qi >= 1) & (dead_ref[0] == 0))
    def _():
        sweep([tile for c in chunks for tile in to_boundary(c)])

    for hh in range(SB_HG):
        lanes = slice(hh * B_HEAD_DIM, (hh + 1) * B_HEAD_DIM)
        gate = gate_ref[:, lanes].astype(F32)
        o_ref[:, lanes] = (acc_ref[hh] * (gate * jax.nn.sigmoid(gate))).astype(BF16)


def _stick_breaking(z):
    width = SB_HG * B_HEAD_DIM
    hcols = B_WIDTH // width
    return pl.pallas_call(
        _sb_kernel,
        grid=(B_HEADS // SB_HG, L_PAD // SB_TQ),
        in_specs=[pl.BlockSpec((SB_TQ, width), lambda h, i: (i, h)),
                  pl.BlockSpec((L_PAD, width), lambda h, i: (0, hcols + h)),
                  pl.BlockSpec((L_PAD, width), lambda h, i: (0, 2 * hcols + h)),
                  pl.BlockSpec((SB_TQ, width), lambda h, i: (i, 3 * hcols + h))],
        out_specs=pl.BlockSpec((SB_TQ, width), lambda h, i: (i, h)),
        out_shape=jax.ShapeDtypeStruct((L_PAD, B_WIDTH), BF16),
        scratch_shapes=[pltpu.VMEM((SB_HG, SB_TQ, B_HEAD_DIM), F32),
                        pltpu.VMEM((SB_HG, SB_TQ, B_HEAD_DIM), F32),
                        pltpu.SMEM((1,), jnp.int32)],
        name="stick_breaking_mixer",
        compiler_params=_params(2),
    )(z, z, z, z)


def _q_scale(n_cols, q_cols, scale):
    s = np.ones((1, n_cols), np.float32)
    s[:, :q_cols] = scale
    return jnp.asarray(s)


def _forward_one(x, meta_tokens, rel_bias, w_in_a, sinks_a, w_out_a, w_in_b, w_out_b, ln_g, ln_b):
    x = x.astype(F32)
    head = jnp.concatenate([jnp.zeros((META_ROW0, D_MODEL), F32), meta_tokens.astype(F32)], axis=0)
    bias = _bias_tables(rel_bias)
    scale_a = _q_scale(A_IN, A_WIDTH, A_HEAD_DIM ** -0.5 * LOG2E)
    scale_b = _q_scale(B_IN, B_WIDTH, B_HEAD_DIM ** -0.5 * LOG2E)
    w_in = (w_in_a.astype(F32), w_in_b.astype(F32))
    w_out = (w_out_a.astype(BF16), w_out_b.astype(BF16))
    in_scale = (scale_a, scale_b)
    in_tn = (1536, 1024)
    h = hb = None
    for i in range(DEPTH):
        j, mixer = i // 2, i % 2
        if i == 0:
            z = _in_proj(x, w_in[mixer], j, in_scale[mixer], FRONT, in_tn[mixer], head=head)
        else:
            z = _in_proj(hb, w_in[mixer], j, in_scale[mixer], 1536, in_tn[mixer])
        gamma, beta = ln_g[i][None, :].astype(F32), ln_b[i][None, :].astype(F32)
        if mixer == 0:
            resid = dict(head=head, x=x) if i == 0 else dict(h=h)
            h, hb = _swa_out_ln(z, sinks_a[j], bias, w_out[mixer], j, gamma, beta, **resid)
        elif i == DEPTH - 1:
            return _out_ln(_stick_breaking(z), w_out[mixer], j, gamma, beta, h, last=True)
        else:
            h, hb = _out_ln(_stick_breaking(z), w_out[mixer], j, gamma, beta, h)


def kernel(x, meta_tokens, rel_bias, w_in_a, sinks_a, w_out_a, w_in_b, w_out_b, ln_g, ln_b):
    batch = x.shape[0]
    assert x.shape[1:] == (SEQ, D_MODEL)
    xs = x.reshape(batch * SEQ, D_MODEL)
    outs = [_forward_one(xs[b * SEQ:(b + 1) * SEQ] if batch > 1 else xs, meta_tokens, rel_bias,
                         w_in_a, sinks_a, w_out_a, w_in_b, w_out_b, ln_g, ln_b) for b in range(batch)]
    out = outs[0] if batch == 1 else jnp.concatenate(outs, axis=0)
    return out.reshape(batch, SEQ, D_MODEL).astype(x.dtype)
```

```python
import functools
import math

import jax
import jax.numpy as jnp
import numpy as np
from jax import lax
from jax.experimental import pallas as pl
from jax.experimental.pallas import tpu as pltpu

F32 = jnp.float32
BF16 = jnp.bfloat16

D_MODEL = 2048
SEQ = 16384
DEPTH = 4
N_META = 16
BLOCK = 128
WINDOW = 128
A_HEAD_DIM = 64
A_Q_HEADS = 32
A_KV_HEADS = 4
A_GROUP = 8
A_WIDTH = 2048
A_KV_WIDTH = 256
A_IN = 2 * A_WIDTH + 2 * A_KV_WIDTH
B_HEAD_DIM = 128
B_HEADS = 16
B_WIDTH = 2048
B_IN = 4 * B_WIDTH
N_BUCKETS = 32
MAX_DISTANCE = 128
ALPHA = (2.0 * DEPTH) ** 0.25
LN_EPS = 1e-5
NEG = -1e30
LOG2E = math.log2(math.e)

FRONT = 512
META_ROW0 = FRONT - N_META
L_PAD = FRONT + SEQ
N_BLK = L_PAD // BLOCK
META_BLK = FRONT // BLOCK - 1
META_LOCAL = BLOCK - N_META

VMEM_LIMIT_BYTES = 56 * 1024 * 1024
VMEM_PLAN_FRACTION = 0.9

LN_TM = 512
SB_TQ = 1536
SB_TK = 256
SB_CH = 256
SB_HG = 2
SB_DEAD = 152.0


def _params(n_axes):
    return pltpu.CompilerParams(dimension_semantics=("arbitrary",) * n_axes,
                                vmem_limit_bytes=VMEM_LIMIT_BYTES)


def _cast_weights(w_ref, scale_ref, wb_ref):
    @pl.when(pl.program_id(1) == 0)
    def _():
        wb_ref[...] = (w_ref[...] * scale_ref[...]).astype(BF16)


def _in_proj_kernel(x_ref, w_ref, scale_ref, o_ref, wb_ref):
    _cast_weights(w_ref, scale_ref, wb_ref)
    o_ref[...] = jnp.dot(x_ref[...], wb_ref[...], preferred_element_type=F32).astype(o_ref.dtype)


def _in_proj_first_kernel(head_ref, x_ref, w_ref, scale_ref, o_ref, wb_ref):
    _cast_weights(w_ref, scale_ref, wb_ref)
    x = jnp.where(pl.program_id(1) == 0, head_ref[...], x_ref[...]).astype(BF16)
    o_ref[...] = jnp.dot(x, wb_ref[...], preferred_element_type=F32).astype(o_ref.dtype)


def _in_proj(x, w_all, layer, scale, tm, tn, head=None):
    _, k, n = w_all.shape
    once = pl.Buffered(1)
    planned = (2 * k * tn * 4 + k * tn * 2 + 2 * tm * k * x.dtype.itemsize + 2 * tm * tn * 2 + tm * tn * 4
               + (0 if head is None else tm * k * (4 + 2)))
    w_mode = pl.Buffered(2 if planned <= VMEM_PLAN_FRACTION * VMEM_LIMIT_BYTES else 1)
    w_spec = pl.BlockSpec((None, k, tn), lambda j, i: (layer, 0, j), pipeline_mode=w_mode)
    s_spec = pl.BlockSpec((1, tn), lambda j, i: (0, j))
    if head is None:
        kern = _in_proj_kernel
        in_specs = [pl.BlockSpec((tm, k), lambda j, i: (i, 0)), w_spec, s_spec]
        args = (x, w_all, scale)
    else:
        assert tm == FRONT
        kern = _in_proj_first_kernel
        in_specs = [pl.BlockSpec((tm, k), lambda j, i: (0, 0), pipeline_mode=once),
                    pl.BlockSpec((tm, k), lambda j, i: (jnp.maximum(i - 1, 0), 0)), w_spec, s_spec]
        args = (head, x, w_all, scale)
    return pl.pallas_call(
        kern,
        grid=(n // tn, L_PAD // tm),
        in_specs=in_specs,
        out_specs=pl.BlockSpec((tm, tn), lambda j, i: (i, j)),
        out_shape=jax.ShapeDtypeStruct((L_PAD, n), BF16),
        scratch_shapes=[pltpu.VMEM((k, tn), BF16)],
        name="in_proj",
        compiler_params=_params(2),
    )(*args)


LN_TILES = L_PAD // LN_TM


LN_CHUNK = 128


def _project_and_norm(g_ref, w_ref, h_prev, gamma_ref, beta_ref, y_refs, out_refs):
    i = pl.program_id(0)

    @pl.when(i == 0)
    def _():
        y_refs[1][...] = jnp.zeros(y_refs[1].shape, F32)

    def step(y_new, y_prev):
        y_new[...] = jnp.dot(g_ref[...], w_ref[...], preferred_element_type=F32)
        for c in range(LN_TM // LN_CHUNK):
            rows = slice(c * LN_CHUNK, (c + 1) * LN_CHUNK)
            t = ALPHA * h_prev(rows) + y_prev[rows]
            mu = jnp.mean(t, axis=-1, keepdims=True)
            d = t - mu
            var = jnp.mean(d * d, axis=-1, keepdims=True)
            out = d * lax.rsqrt(var + LN_EPS) * gamma_ref[...] + beta_ref[...]
            for ref in out_refs:
                ref[rows] = out.astype(ref.dtype)

    for parity in range(2):
        pl.when(i % 2 == parity)(functools.partial(step, y_refs[parity], y_refs[1 - parity]))


def _out_ln_mid_kernel(g_ref, w_ref, h_ref, gamma_ref, beta_ref, hf_ref, hb_ref, ya_ref, yb_ref):
    _project_and_norm(g_ref, w_ref, lambda rows: h_ref[rows], gamma_ref, beta_ref, (ya_ref, yb_ref),
                      (hf_ref, hb_ref))


def _out_ln_last_kernel(g_ref, w_ref, h_ref, gamma_ref, beta_ref, o_ref, ya_ref, yb_ref):
    _project_and_norm(g_ref, w_ref, lambda rows: h_ref[rows], gamma_ref, beta_ref, (ya_ref, yb_ref), (o_ref,))


def _out_ln(g, w_all, layer, gamma, beta, h, last=False):
    d = D_MODEL
    assert FRONT == LN_TM
    fixed = lambda i: (0, 0)
    prev = lambda i: (jnp.maximum(i - 1, 0), 0)
    prev_tok = lambda i: (jnp.maximum(i - 2, 0), 0)
    tile = lambda imap: pl.BlockSpec((LN_TM, d), imap)
    if last:
        kern = _out_ln_last_kernel
        outs = dict(out_specs=tile(prev_tok), out_shape=jax.ShapeDtypeStruct((SEQ, d), F32))
    else:
        kern = _out_ln_mid_kernel
        outs = dict(out_specs=[tile(prev), tile(prev)],
                    out_shape=[jax.ShapeDtypeStruct((L_PAD, d), F32), jax.ShapeDtypeStruct((L_PAD, d), BF16)])
    return pl.pallas_call(
        kern,
        grid=(LN_TILES + 1,),
        in_specs=[tile(lambda i: (jnp.minimum(i, LN_TILES - 1), 0)),
                  pl.BlockSpec((None, d, d), lambda i: (layer, 0, 0), pipeline_mode=pl.Buffered(1)),
                  tile(prev), pl.BlockSpec((1, d), fixed), pl.BlockSpec((1, d), fixed)],
        scratch_shapes=[pltpu.VMEM((LN_TM, d), F32), pltpu.VMEM((LN_TM, d), F32)],
        name="out_proj_ln",
        compiler_params=_params(1),
        **outs,
    )(g, w_all, h, gamma, beta)


def _t5_bucket_np(dist):
    max_exact = N_BUCKETS // 2
    d = np.maximum(dist, 0)
    df = np.maximum(d, 1).astype(np.float32)
    large = max_exact + (np.log(df / np.float32(max_exact)) / np.float32(math.log(MAX_DISTANCE / max_exact))
                         * np.float32(N_BUCKETS - max_exact)).astype(np.int32)
    large = np.minimum(large, N_BUCKETS - 1)
    return np.where(d < max_exact, d, large).astype(np.int32)


def _bucket_maps():
    masked = N_BUCKETS
    r = np.arange(BLOCK)[:, None]
    c = np.arange(BLOCK)[None, :]
    out = np.full((3, BLOCK, 2 * BLOCK), masked, np.int32)
    ok = (r >= META_LOCAL) & (c >= META_LOCAL) & (r - c >= 0)
    out[0, :, :BLOCK] = np.where(ok, _t5_bucket_np(r - c), masked)
    own = _t5_bucket_np(r - c)
    prev = _t5_bucket_np(r + BLOCK - c)
    out[1, :, :BLOCK] = np.where(c >= META_LOCAL, _t5_bucket_np(r + BLOCK - c), masked)
    out[1, :, BLOCK:] = np.where(c <= r, own, masked)
    far = _t5_bucket_np(np.array(2 * BLOCK + 1))
    out[2, :, :BLOCK] = np.where(c >= META_LOCAL, far, masked) + 0 * r
    out[2, :, BLOCK:] = np.where(c <= r, own, prev)
    return out


def _bias_kernel(rel_ref, map_ref, o_ref):
    bmap = map_ref[0]

    def one_head(h, carry):
        acc = jnp.full(bmap.shape, NEG, F32)
        for b in range(N_BUCKETS):
            acc = jnp.where(bmap == b, rel_ref[b, h] * LOG2E, acc)
        o_ref[0, h] = acc
        return carry

    lax.fori_loop(0, A_Q_HEADS, one_head, 0)


def _bias_tables(rel_bias):
    maps = jnp.asarray(_bucket_maps())
    return pl.pallas_call(
        _bias_kernel,
        grid=(3,),
        in_specs=[pl.BlockSpec(memory_space=pltpu.SMEM),
                  pl.BlockSpec((1, BLOCK, 2 * BLOCK), lambda c: (c, 0, 0))],
        out_specs=pl.BlockSpec((1, A_Q_HEADS, BLOCK, 2 * BLOCK), lambda c: (c, 0, 0, 0)),
        out_shape=jax.ShapeDtypeStruct((3, A_Q_HEADS, BLOCK, 2 * BLOCK), F32),
        name="bias_tables",
        compiler_params=_params(1),
    )(rel_bias.astype(F32), maps)


SWA_TILES = A_Q_HEADS // 2
SWA_AHEAD = 2


def _swa_block(sink_ref, zc_ref, zp_ref, zm_ref, bias_ref, o_ref, between=None):
    kv0 = A_WIDTH
    kcat = jnp.concatenate([zm_ref[:, :A_KV_WIDTH], zp_ref[:, :A_KV_WIDTH],
                            zc_ref[:, kv0:kv0 + A_KV_WIDTH]], axis=0)
    vcat = jnp.concatenate([zm_ref[:, A_KV_WIDTH:], zp_ref[:, A_KV_WIDTH:],
                            zc_ref[:, kv0 + A_KV_WIDTH:kv0 + 2 * A_KV_WIDTH]], axis=0)
    gate0 = kv0 + 2 * A_KV_WIDTH
    lane_k = lax.broadcasted_iota(jnp.int32, (3 * BLOCK, BLOCK), 1) < A_HEAD_DIM
    first = lax.broadcasted_iota(jnp.int32, (BLOCK, BLOCK), 1) < A_HEAD_DIM
    own = (lax.broadcasted_iota(jnp.int32, (BLOCK, BLOCK), 0)
           >= lax.broadcasted_iota(jnp.int32, (BLOCK, BLOCK), 1))
    tiles_per_group = A_GROUP // 2
    n_tiles = SWA_TILES
    kv_cache = {}

    def kv_of(g):
        if g not in kv_cache:
            pair = slice((g // 2) * BLOCK, (g // 2 + 1) * BLOCK)
            k2 = kcat[:, pair].astype(F32)
            v2 = vcat[:, pair].astype(F32)
            k2r = pltpu.roll(k2, A_HEAD_DIM, axis=1)
            v2r = pltpu.roll(v2, A_HEAD_DIM, axis=1)
            if g % 2 == 0:
                kd, v_lo, v_hi = jnp.where(lane_k, k2, k2r), v2, v2r
            else:
                kd, v_lo, v_hi = jnp.where(lane_k, k2r, k2), v2r, v2
            vd = [jnp.where(lane_k, v_lo, 1.0).astype(BF16), jnp.where(lane_k, 1.0, v_hi).astype(BF16)]
            kv_cache[g] = (kd.astype(BF16), vd)
        return kv_cache[g]

    def scores(tile):
        qt = zc_ref[:, tile * BLOCK:(tile + 1) * BLOCK]
        zero = jnp.zeros_like(qt)
        qq = jnp.concatenate([jnp.where(first, qt, zero), jnp.where(first, zero, qt)], axis=0)
        kd, _ = kv_of(tile // tiles_per_group)
        return lax.dot_general(qq, kd, (((1,), (1,)), ((), ())), preferred_element_type=F32)

    def finish(tile, res, sink_term):
        cols = slice(tile * BLOCK, (tile + 1) * BLOCK)
        num = jnp.where(first, res[0], res[1])
        den = (pltpu.roll(jnp.where(first, res[1], res[0]), A_HEAD_DIM, axis=1)
               + jnp.where(first, sink_term[0], sink_term[1]))
        gate = zc_ref[:, gate0 + tile * BLOCK:gate0 + (tile + 1) * BLOCK].astype(F32)
        o_ref[:, cols] = (num / den * (gate * jax.nn.sigmoid(gate))).astype(BF16)

    pending = [scores(t) for t in range(SWA_AHEAD)]
    unfinished = None
    for tile in range(n_tiles):
        s = pending.pop(0)
        if tile + SWA_AHEAD < n_tiles:
            pending.append(scores(tile + SWA_AHEAD))
        if between is not None:
            between(tile)
        _, vd = kv_of(tile // tiles_per_group)
        res, sink_term = [], []
        for e in range(2):
            head = 2 * tile + e
            se = s[e * BLOCK:(e + 1) * BLOCK]
            window = jnp.where(own, se[:, 2 * BLOCK:], se[:, BLOCK:2 * BLOCK])
            sc = jnp.concatenate([se[:, :BLOCK], window], axis=1) + bias_ref[0, head]
            sink = sink_ref[head] * LOG2E
            m = jnp.maximum(jnp.max(sc, axis=-1, keepdims=True), sink)
            p = jnp.exp2(sc - m)
            pw = p[:, BLOCK:]
            pf = jnp.concatenate([p[:, :BLOCK], jnp.where(own, 0.0, pw), jnp.where(own, pw, 0.0)],
                                 axis=1).astype(BF16)
            res.append(jnp.dot(pf, vd[e], preferred_element_type=F32))
            sink_term.append(jnp.exp2(sink - m))
        if unfinished is not None:
            finish(*unfinished)
        unfinished = (tile, res, sink_term)
    finish(*unfinished)


OUT_SLABS = 4
NORM_ROWS = BLOCK // (SWA_TILES - 2 * OUT_SLABS)


def _swa_deepnorm(n, attn_refs, w_ref, h_prev, gamma_ref, beta_ref, hf_ref, hb_ref, g_refs, y_ref):
    @pl.when(n == 0)
    def _():
        g_refs[1][...] = jnp.zeros(g_refs[1].shape, BF16)

    def step(g_new, g_prev):
        slab = D_MODEL // OUT_SLABS

        def between(tile):
            if tile % 2 == 0 and tile < 2 * OUT_SLABS:
                cols = slice((tile // 2) * slab, (tile // 2 + 1) * slab)
                y_ref[:, cols] = jnp.dot(g_prev[...], w_ref[:, cols], preferred_element_type=F32)
            elif tile >= 2 * OUT_SLABS:
                r = tile - 2 * OUT_SLABS
                rows = slice(r * NORM_ROWS, (r + 1) * NORM_ROWS)
                t = ALPHA * h_prev(rows) + y_ref[rows]
                mu = jnp.mean(t, axis=-1, keepdims=True)
                d = t - mu
                var = jnp.mean(d * d, axis=-1, keepdims=True)
                out = d * lax.rsqrt(var + LN_EPS) * gamma_ref[...] + beta_ref[...]
                hf_ref[rows] = out
                hb_ref[rows] = out.astype(BF16)

        _swa_block(*attn_refs, g_new, between)

    for parity in range(2):
        pl.when(n % 2 == parity)(functools.partial(step, g_refs[parity], g_refs[1 - parity]))


def _swa_deepnorm_first_kernel(sink_ref, zc_ref, zp_ref, zm_ref, bias_ref, w_ref, head_ref, x_ref,
                               gamma_ref, beta_ref, hf_ref, hb_ref, ga_ref, gb_ref, y_ref):
    n = pl.program_id(0)
    in_front = n - 1 < FRONT // BLOCK
    h_prev = lambda rows: jnp.where(in_front, head_ref[rows], x_ref[rows])
    _swa_deepnorm(n, (sink_ref, zc_ref, zp_ref, zm_ref, bias_ref), w_ref, h_prev, gamma_ref, beta_ref,
                  hf_ref, hb_ref, (ga_ref, gb_ref), y_ref)


def _swa_deepnorm_mid_kernel(sink_ref, zc_ref, zp_ref, zm_ref, bias_ref, w_ref, h_ref,
                             gamma_ref, beta_ref, hf_ref, hb_ref, ga_ref, gb_ref, y_ref):
    _swa_deepnorm(pl.program_id(0), (sink_ref, zc_ref, zp_ref, zm_ref, bias_ref), w_ref,
                  lambda rows: h_ref[rows], gamma_ref, beta_ref, hf_ref, hb_ref, (ga_ref, gb_ref), y_ref)


def _swa_out_ln(z, sinks, bias, w_all, layer, gamma, beta, *, h=None, head=None, x=None):
    d = D_MODEL
    kv_tile = A_WIDTH // (2 * A_KV_WIDTH)
    front_blocks = FRONT // BLOCK
    once = pl.Buffered(1)
    cur = lambda n: jnp.minimum(n, N_BLK - 1)
    prev = lambda n: (jnp.maximum(n - 1, 0), 0)
    fixed = lambda n: (0, 0)
    rows = lambda imap: pl.BlockSpec((BLOCK, d), imap)
    attn = [pl.BlockSpec(memory_space=pltpu.SMEM),
            pl.BlockSpec((BLOCK, A_IN), lambda n: (cur(n), 0)),
            pl.BlockSpec((BLOCK, 2 * A_KV_WIDTH), lambda n: (jnp.maximum(cur(n) - 1, 0), kv_tile)),
            pl.BlockSpec((BLOCK, 2 * A_KV_WIDTH), lambda n: (META_BLK, kv_tile)),
            pl.BlockSpec((1, A_Q_HEADS, BLOCK, 2 * BLOCK),
                         lambda n: (jnp.clip(cur(n) - META_BLK, 0, 2), 0, 0, 0)),
            pl.BlockSpec((None, d, d), lambda n: (layer, 0, 0), pipeline_mode=once)]
    if head is not None:
        kern, args = _swa_deepnorm_first_kernel, (head, x)
        resid = [rows(lambda n: (jnp.clip(n - 1, 0, front_blocks - 1), 0)),
                 rows(lambda n: (jnp.maximum(n - 1 - front_blocks, 0), 0))]
    else:
        kern, args, resid = _swa_deepnorm_mid_kernel, (h,), [rows(prev)]
    return pl.pallas_call(
        kern,
        grid=(N_BLK + 1,),
        in_specs=attn + resid + [pl.BlockSpec((1, d), fixed), pl.BlockSpec((1, d), fixed)],
        out_specs=[rows(prev), rows(prev)],
        out_shape=[jax.ShapeDtypeStruct((L_PAD, d), F32), jax.ShapeDtypeStruct((L_PAD, d), BF16)],
        scratch_shapes=[pltpu.VMEM((BLOCK, A_WIDTH), BF16), pltpu.VMEM((BLOCK, A_WIDTH), BF16),
                        pltpu.VMEM((BLOCK, d), F32)],
        name="swa_mixer_out_proj_ln",
        compiler_params=_params(1),
    )(sinks.astype(F32), z, z, z, bias, w_all, *args, gamma, beta)


SOFTPLUS2_DIRECT_MAX = 126.0


def _softplus2(s):
    return jnp.maximum(s, jnp.log2(1.0 + jnp.exp2(jnp.minimum(s, SOFTPLUS2_DIRECT_MAX))))


def _sb_kernel(q_ref, k_ref, v_ref, gate_ref, o_ref, acc_ref, later_ref, dead_ref):
    qi = pl.program_id(1)

    def iotas(width):
        return (lax.broadcasted_iota(jnp.int32, (SB_CH, width), 0),
                lax.broadcasted_iota(jnp.int32, (SB_CH, width), 1))

    suffix = {w: (lax.broadcasted_iota(jnp.int32, (w, w), 0)
                  >= lax.broadcasted_iota(jnp.int32, (w, w), 1)).astype(BF16) for w in (SB_CH, SB_TK)}
    rr, cc = iotas(SB_CH)
    _, cc_wide = iotas(SB_TK)
    pad_col = lambda start: META_ROW0 - start
    n_chunks = SB_TQ // SB_CH

    def visible(kind, start, width):
        if kind is None:
            return None
        if kind == "tri":
            return cc < rr
        if kind == "tri_pad":
            return (cc < rr) & (cc >= pad_col(start))
        assert kind == "pad" and width == SB_TK
        return cc_wide >= pad_col(start)

    def sweep(spans, first=False):
        items = [(start, width, visible(kind, start, width), c, hh)
                 for (start, width, kind, c) in spans for hh in range(SB_HG)]
        rows = lambda c: slice(c * SB_CH, (c + 1) * SB_CH)
        lanes = lambda hh: slice(hh * B_HEAD_DIM, (hh + 1) * B_HEAD_DIM)
        keys = lambda start, width: pl.ds(start if isinstance(start, int) else pl.multiple_of(start, SB_CH), width)
        later = {(c, hh): (jnp.zeros((SB_CH, B_HEAD_DIM), F32) if first else later_ref[hh, rows(c)])
                 for (_, _, _, c) in spans for hh in range(SB_HG)}
        acc, s, tw, later_before = {}, {}, {}, {}

        def stage_scores(n):
            start, width, _, c, hh = items[n]
            k = k_ref[keys(start, width), lanes(hh)]
            s[n] = lax.dot_general(q_ref[rows(c), lanes(hh)], k, (((1,), (1,)), ((), ())),
                                   preferred_element_type=F32)

        def stage_suffix(n):
            _, width, vis, c, hh = items[n]
            sp = _softplus2(s[n])
            if vis is not None:
                sp = jnp.where(vis, sp, 0.0)
            tw[n] = jnp.dot(sp.astype(BF16), suffix[width], preferred_element_type=F32)
            tot = jnp.sum(sp, axis=-1, keepdims=True)
            later_before[n] = later[(c, hh)]
            later[(c, hh)] = later_before[n] + jnp.broadcast_to(tot, later_before[n].shape)

        def stage_values(n):
            start, width, vis, c, hh = items[n]
            lat = jnp.concatenate([later_before.pop(n)] * (width // B_HEAD_DIM), axis=1)
            p = jnp.exp2(s.pop(n) - tw.pop(n) - lat)
            if vis is not None:
                p = jnp.where(vis, p, 0.0)
            v = v_ref[keys(start, width), lanes(hh)]
            pv = jnp.dot(p.astype(BF16), v, preferred_element_type=F32)
            acc[(c, hh)] = acc[(c, hh)] + pv if (c, hh) in acc else pv

        skew = 2
        for step in range(len(items) + 2 * skew):
            if step < len(items):
                stage_scores(step)
            if 0 <= step - skew < len(items):
                stage_suffix(step - skew)
                if step - skew == len(items) - 1:
                    least = functools.reduce(jnp.minimum, later.values())
                    dead_ref[0] = (jnp.min(least) >= SB_DEAD).astype(jnp.int32)
            if 0 <= step - 2 * skew < len(items):
                stage_values(step - 2 * skew)
        for (c, hh), a in acc.items():
            if first:
                acc_ref[hh, rows(c)] = a
            else:
                acc_ref[hh, rows(c)] += a
            later_ref[hh, rows(c)] = later[(c, hh)]

    assert SB_TQ % SB_TK == 0 and FRONT % SB_TK == 0
    chunks = range(n_chunks)
    own = lambda c: qi * SB_TQ + c * SB_CH

    def to_boundary(start, c):
        full = [(st, SB_TK, None, c) for st in range(start, FRONT - 1, -SB_TK)]
        last = full[-1][0] - SB_TK if full else start
        assert last <= META_ROW0 < last + SB_TK
        return full + [(last, SB_TK, "pad", c)]

    @pl.when(qi == 0)
    def _():
        meta_chunk = META_ROW0 // SB_CH
        acc_ref[:, :meta_chunk * SB_CH] = jnp.zeros((SB_HG, meta_chunk * SB_CH, B_HEAD_DIM), F32)
        spans = [(meta_chunk * SB_CH, SB_CH, "tri_pad", meta_chunk)]
        for c in chunks[meta_chunk + 1:]:
            spans += [(c * SB_CH, SB_CH, "tri", c)] + to_boundary(c * SB_CH - SB_TK, c)
        sweep(spans, first=True)

    @pl.when(qi >= 1)
    def _():
        sweep([(own(c), SB_CH, "tri", c) for c in chunks]
              + [(own(c) - SB_TK, SB_TK, None, c) for c in chunks], first=True)

    def body(t):
        sweep([(own(c) - t * SB_TK, SB_TK, None, c) for c in chunks])
        return t + 1

    lax.while_loop(lambda t: (own(0) - t * SB_TK >= FRONT) & (dead_ref[0] == 0), body, jnp.int32(2))

    @pl.when((qi >= 1) & (dead_ref[0] == 0))
    def _():
        sweep([span for c in chunks for span in to_boundary(FRONT - SB_TK + c * SB_CH, c)])

    for hh in range(SB_HG):
        lanes = slice(hh * B_HEAD_DIM, (hh + 1) * B_HEAD_DIM)
        gate = gate_ref[:, lanes].astype(F32)
        o_ref[:, lanes] = (acc_ref[hh] * (gate * jax.nn.sigmoid(gate))).astype(BF16)


def _stick_breaking(z):
    width = SB_HG * B_HEAD_DIM
    hcols = B_WIDTH // width
    return pl.pallas_call(
        _sb_kernel,
        grid=(B_HEADS // SB_HG, L_PAD // SB_TQ),
        in_specs=[pl.BlockSpec((SB_TQ, width), lambda h, i: (i, h)),
                  pl.BlockSpec((L_PAD, width), lambda h, i: (0, hcols + h)),
                  pl.BlockSpec((L_PAD, width), lambda h, i: (0, 2 * hcols + h)),
                  pl.BlockSpec((SB_TQ, width), lambda h, i: (i, 3 * hcols + h))],
        out_specs=pl.BlockSpec((SB_TQ, width), lambda h, i: (i, h)),
        out_shape=jax.ShapeDtypeStruct((L_PAD, B_WIDTH), BF16),
        scratch_shapes=[pltpu.VMEM((SB_HG, SB_TQ, B_HEAD_DIM), F32),
                        pltpu.VMEM((SB_HG, SB_TQ, B_HEAD_DIM), F32),
                        pltpu.SMEM((1,), jnp.int32)],
        name="stick_breaking_mixer",
        compiler_params=_params(2),
    )(z, z, z, z)


def _q_scale(n_cols, q_cols, scale):
    s = np.ones((1, n_cols), np.float32)
    s[:, :q_cols] = scale
    return jnp.asarray(s)


def _forward_one(x, meta_tokens, rel_bias, w_in_a, sinks_a, w_out_a, w_in_b, w_out_b, ln_g, ln_b):
    x = x.astype(F32)
    head = jnp.concatenate([jnp.zeros((META_ROW0, D_MODEL), F32), meta_tokens.astype(F32)], axis=0)
    bias = _bias_tables(rel_bias)
    scale_a = _q_scale(A_IN, A_WIDTH, A_HEAD_DIM ** -0.5 * LOG2E)
    scale_b = _q_scale(B_IN, B_WIDTH, B_HEAD_DIM ** -0.5 * LOG2E)
    w_in = (w_in_a.astype(F32), w_in_b.astype(F32))
    w_out = (w_out_a.astype(BF16), w_out_b.astype(BF16))
    in_scale = (scale_a, scale_b)
    in_tn = (1536, 1024)
    h = hb = None
    for i in range(DEPTH):
        j, mixer = i // 2, i % 2
        if i == 0:
            z = _in_proj(x, w_in[mixer], j, in_scale[mixer], FRONT, in_tn[mixer], head=head)
        else:
            z = _in_proj(hb, w_in[mixer], j, in_scale[mixer], 1536, in_tn[mixer])
        gamma, beta = ln_g[i][None, :].astype(F32), ln_b[i][None, :].astype(F32)
        if mixer == 0:
            resid = dict(head=head, x=x) if i == 0 else dict(h=h)
            h, hb = _swa_out_ln(z, sinks_a[j], bias, w_out[mixer], j, gamma, beta, **resid)
        elif i == DEPTH - 1:
            return _out_ln(_stick_breaking(z), w_out[mixer], j, gamma, beta, h, last=True)
        else:
            h, hb = _out_ln(_stick_breaking(z), w_out[mixer], j, gamma, beta, h)


def kernel(x, meta_tokens, rel_bias, w_in_a, sinks_a, w_out_a, w_in_b, w_out_b, ln_g, ln_b):
    batch = x.shape[0]
    assert x.shape[1:] == (SEQ, D_MODEL)
    xs = x.reshape(batch * SEQ, D_MODEL)
    outs = [_forward_one(xs[b * SEQ:(b + 1) * SEQ] if batch > 1 else xs, meta_tokens, rel_bias,
                         w_in_a, sinks_a, w_out_a, w_in_b, w_out_b, ln_g, ln_b) for b in range(batch)]
    out = outs[0] if batch == 1 else jnp.concatenate(outs, axis=0)
    return out.reshape(batch, SEQ, D_MODEL).astype(x.dtype)
```

```python
import functools
import math

import jax
import jax.numpy as jnp
import numpy as np
from jax import lax
from jax.experimental import pallas as pl
from jax.experimental.pallas import tpu as pltpu

F32 = jnp.float32
BF16 = jnp.bfloat16

D_MODEL = 2048
SEQ = 16384
DEPTH = 4
N_META = 16
BLOCK = 128
WINDOW = 128
A_HEAD_DIM = 64
A_Q_HEADS = 32
A_KV_HEADS = 4
A_GROUP = 8
A_WIDTH = 2048
A_KV_WIDTH = 256
A_IN = 2 * A_WIDTH + 2 * A_KV_WIDTH
B_HEAD_DIM = 128
B_HEADS = 16
B_WIDTH = 2048
B_IN = 4 * B_WIDTH
N_BUCKETS = 32
MAX_DISTANCE = 128
ALPHA = (2.0 * DEPTH) ** 0.25
LN_EPS = 1e-5
NEG = -1e30
LOG2E = math.log2(math.e)

FRONT = 512
META_ROW0 = FRONT - N_META
L_PAD = FRONT + SEQ
N_BLK = L_PAD // BLOCK
META_BLK = FRONT // BLOCK - 1
META_LOCAL = BLOCK - N_META

VMEM_LIMIT_BYTES = 56 * 1024 * 1024
VMEM_PLAN_FRACTION = 0.9

LN_TM = 512
SB_TQ = 1536
SB_TK = 256
SB_CH = 256
SB_HG = 2
SB_DEAD = 152.0


def _params(n_axes):
    return pltpu.CompilerParams(dimension_semantics=("arbitrary",) * n_axes,
                                vmem_limit_bytes=VMEM_LIMIT_BYTES)


def _cast_weights(w_ref, scale_ref, wb_ref):
    @pl.when(pl.program_id(1) == 0)
    def _():
        wb_ref[...] = (w_ref[...] * scale_ref[...]).astype(BF16)


def _in_proj_kernel(x_ref, w_ref, scale_ref, o_ref, wb_ref):
    _cast_weights(w_ref, scale_ref, wb_ref)
    o_ref[...] = jnp.dot(x_ref[...], wb_ref[...], preferred_element_type=F32).astype(o_ref.dtype)


def _in_proj_first_kernel(head_ref, x_ref, w_ref, scale_ref, o_ref, wb_ref):
    _cast_weights(w_ref, scale_ref, wb_ref)
    x = jnp.where(pl.program_id(1) == 0, head_ref[...], x_ref[...]).astype(BF16)
    o_ref[...] = jnp.dot(x, wb_ref[...], preferred_element_type=F32).astype(o_ref.dtype)


def _in_proj(x, w_all, layer, scale, tm, tn, head=None):
    _, k, n = w_all.shape
    once = pl.Buffered(1)
    planned = (2 * k * tn * 4 + k * tn * 2 + 2 * tm * k * x.dtype.itemsize + 2 * tm * tn * 2 + tm * tn * 4
               + (0 if head is None else tm * k * (4 + 2)))
    w_mode = pl.Buffered(2 if planned <= VMEM_PLAN_FRACTION * VMEM_LIMIT_BYTES else 1)
    w_spec = pl.BlockSpec((None, k, tn), lambda j, i: (layer, 0, j), pipeline_mode=w_mode)
    s_spec = pl.BlockSpec((1, tn), lambda j, i: (0, j))
    if head is None:
        kern = _in_proj_kernel
        in_specs = [pl.BlockSpec((tm, k), lambda j, i: (i, 0)), w_spec, s_spec]
        args = (x, w_all, scale)
    else:
        assert tm == FRONT
        kern = _in_proj_first_kernel
        in_specs = [pl.BlockSpec((tm, k), lambda j, i: (0, 0), pipeline_mode=once),
                    pl.BlockSpec((tm, k), lambda j, i: (jnp.maximum(i - 1, 0), 0)), w_spec, s_spec]
        args = (head, x, w_all, scale)
    return pl.pallas_call(
        kern,
        grid=(n // tn, L_PAD // tm),
        in_specs=in_specs,
        out_specs=pl.BlockSpec((tm, tn), lambda j, i: (i, j)),
        out_shape=jax.ShapeDtypeStruct((L_PAD, n), BF16),
        scratch_shapes=[pltpu.VMEM((k, tn), BF16)],
        name="in_proj",
        compiler_params=_params(2),
    )(*args)


LN_TILES = L_PAD // LN_TM


LN_CHUNK = 128


def _project_and_norm(g_ref, w_ref, h_prev, gamma_ref, beta_ref, y_refs, out_refs):
    i = pl.program_id(0)

    @pl.when(i == 0)
    def _():
        y_refs[1][...] = jnp.zeros(y_refs[1].shape, F32)

    def step(y_new, y_prev):
        y_new[...] = jnp.dot(g_ref[...], w_ref[...], preferred_element_type=F32)
        for c in range(LN_TM // LN_CHUNK):
            rows = slice(c * LN_CHUNK, (c + 1) * LN_CHUNK)
            t = ALPHA * h_prev(rows) + y_prev[rows]
            mu = jnp.mean(t, axis=-1, keepdims=True)
            d = t - mu
            var = jnp.mean(d * d, axis=-1, keepdims=True)
            out = d * lax.rsqrt(var + LN_EPS) * gamma_ref[...] + beta_ref[...]
            for ref in out_refs:
                ref[rows] = out.astype(ref.dtype)

    for parity in range(2):
        pl.when(i % 2 == parity)(functools.partial(step, y_refs[parity], y_refs[1 - parity]))


def _out_ln_mid_kernel(g_ref, w_ref, h_ref, gamma_ref, beta_ref, hf_ref, hb_ref, ya_ref, yb_ref):
    _project_and_norm(g_ref, w_ref, lambda rows: h_ref[rows], gamma_ref, beta_ref, (ya_ref, yb_ref),
                      (hf_ref, hb_ref))


def _out_ln_last_kernel(g_ref, w_ref, h_ref, gamma_ref, beta_ref, o_ref, ya_ref, yb_ref):
    _project_and_norm(g_ref, w_ref, lambda rows: h_ref[rows], gamma_ref, beta_ref, (ya_ref, yb_ref), (o_ref,))


def _out_ln(g, w_all, layer, gamma, beta, h, last=False):
    d = D_MODEL
    assert FRONT == LN_TM
    fixed = lambda i: (0, 0)
    prev = lambda i: (jnp.maximum(i - 1, 0), 0)
    prev_tok = lambda i: (jnp.maximum(i - 2, 0), 0)
    tile = lambda imap: pl.BlockSpec((LN_TM, d), imap)
    if last:
        kern = _out_ln_last_kernel
        outs = dict(out_specs=tile(prev_tok), out_shape=jax.ShapeDtypeStruct((SEQ, d), F32))
    else:
        kern = _out_ln_mid_kernel
        outs = dict(out_specs=[tile(prev), tile(prev)],
                    out_shape=[jax.ShapeDtypeStruct((L_PAD, d), F32), jax.ShapeDtypeStruct((L_PAD, d), BF16)])
    return pl.pallas_call(
        kern,
        grid=(LN_TILES + 1,),
        in_specs=[tile(lambda i: (jnp.minimum(i, LN_TILES - 1), 0)),
                  pl.BlockSpec((None, d, d), lambda i: (layer, 0, 0), pipeline_mode=pl.Buffered(1)),
                  tile(prev), pl.BlockSpec((1, d), fixed), pl.BlockSpec((1, d), fixed)],
        scratch_shapes=[pltpu.VMEM((LN_TM, d), F32), pltpu.VMEM((LN_TM, d), F32)],
        name="out_proj_ln",
        compiler_params=_params(1),
        **outs,
    )(g, w_all, h, gamma, beta)


def _t5_bucket_np(dist):
    max_exact = N_BUCKETS // 2
    d = np.maximum(dist, 0)
    df = np.maximum(d, 1).astype(np.float32)
    large = max_exact + (np.log(df / np.float32(max_exact)) / np.float32(math.log(MAX_DISTANCE / max_exact))
                         * np.float32(N_BUCKETS - max_exact)).astype(np.int32)
    large = np.minimum(large, N_BUCKETS - 1)
    return np.where(d < max_exact, d, large).astype(np.int32)


def _bucket_maps():
    masked = N_BUCKETS
    r = np.arange(BLOCK)[:, None]
    c = np.arange(BLOCK)[None, :]
    out = np.full((3, BLOCK, 2 * BLOCK), masked, np.int32)
    ok = (r >= META_LOCAL) & (c >= META_LOCAL) & (r - c >= 0)
    out[0, :, :BLOCK] = np.where(ok, _t5_bucket_np(r - c), masked)
    own = _t5_bucket_np(r - c)
    prev = _t5_bucket_np(r + BLOCK - c)
    out[1, :, :BLOCK] = np.where(c >= META_LOCAL, _t5_bucket_np(r + BLOCK - c), masked)
    out[1, :, BLOCK:] = np.where(c <= r, own, masked)
    far = _t5_bucket_np(np.array(2 * BLOCK + 1))
    out[2, :, :BLOCK] = np.where(c >= META_LOCAL, far, masked) + 0 * r
    out[2, :, BLOCK:] = np.where(c <= r, own, prev)
    return out


def _bias_kernel(rel_ref, map_ref, o_ref):
    bmap = map_ref[0]

    def one_head(h, carry):
        acc = jnp.full(bmap.shape, NEG, F32)
        for b in range(N_BUCKETS):
            acc = jnp.where(bmap == b, rel_ref[b, h] * LOG2E, acc)
        o_ref[0, h] = acc
        return carry

    lax.fori_loop(0, A_Q_HEADS, one_head, 0)


def _bias_tables(rel_bias):
    maps = jnp.asarray(_bucket_maps())
    return pl.pallas_call(
        _bias_kernel,
        grid=(3,),
        in_specs=[pl.BlockSpec(memory_space=pltpu.SMEM),
                  pl.BlockSpec((1, BLOCK, 2 * BLOCK), lambda c: (c, 0, 0))],
        out_specs=pl.BlockSpec((1, A_Q_HEADS, BLOCK, 2 * BLOCK), lambda c: (c, 0, 0, 0)),
        out_shape=jax.ShapeDtypeStruct((3, A_Q_HEADS, BLOCK, 2 * BLOCK), F32),
        name="bias_tables",
        compiler_params=_params(1),
    )(rel_bias.astype(F32), maps)


SWA_TILES = A_Q_HEADS // 2
SWA_AHEAD = 2


def _swa_block(sink_ref, zc_ref, zp_ref, zm_ref, bias_ref, o_ref, between=None):
    kv0 = A_WIDTH
    kcat = jnp.concatenate([zm_ref[:, :A_KV_WIDTH], zp_ref[:, :A_KV_WIDTH],
                            zc_ref[:, kv0:kv0 + A_KV_WIDTH]], axis=0)
    vcat = jnp.concatenate([zm_ref[:, A_KV_WIDTH:], zp_ref[:, A_KV_WIDTH:],
                            zc_ref[:, kv0 + A_KV_WIDTH:kv0 + 2 * A_KV_WIDTH]], axis=0)
    gate0 = kv0 + 2 * A_KV_WIDTH
    lane_k = lax.broadcasted_iota(jnp.int32, (3 * BLOCK, BLOCK), 1) < A_HEAD_DIM
    first = lax.broadcasted_iota(jnp.int32, (BLOCK, BLOCK), 1) < A_HEAD_DIM
    own = (lax.broadcasted_iota(jnp.int32, (BLOCK, BLOCK), 0)
           >= lax.broadcasted_iota(jnp.int32, (BLOCK, BLOCK), 1))
    tiles_per_group = A_GROUP // 2
    n_tiles = SWA_TILES
    kv_cache = {}

    def kv_of(g):
        if g not in kv_cache:
            pair = slice((g // 2) * BLOCK, (g // 2 + 1) * BLOCK)
            k2 = kcat[:, pair].astype(F32)
            v2 = vcat[:, pair].astype(F32)
            k2r = pltpu.roll(k2, A_HEAD_DIM, axis=1)
            v2r = pltpu.roll(v2, A_HEAD_DIM, axis=1)
            if g % 2 == 0:
                kd, v_lo, v_hi = jnp.where(lane_k, k2, k2r), v2, v2r
            else:
                kd, v_lo, v_hi = jnp.where(lane_k, k2r, k2), v2r, v2
            vd = [jnp.where(lane_k, v_lo, 1.0).astype(BF16), jnp.where(lane_k, 1.0, v_hi).astype(BF16)]
            kv_cache[g] = (kd.astype(BF16), vd)
        return kv_cache[g]

    def scores(tile):
        qt = zc_ref[:, tile * BLOCK:(tile + 1) * BLOCK]
        zero = jnp.zeros_like(qt)
        qq = jnp.concatenate([jnp.where(first, qt, zero), jnp.where(first, zero, qt)], axis=0)
        kd, _ = kv_of(tile // tiles_per_group)
        return lax.dot_general(qq, kd, (((1,), (1,)), ((), ())), preferred_element_type=F32)

    def finish(tile, res, sink_term):
        cols = slice(tile * BLOCK, (tile + 1) * BLOCK)
        num = jnp.where(first, res[0], res[1])
        den = (pltpu.roll(jnp.where(first, res[1], res[0]), A_HEAD_DIM, axis=1)
               + jnp.where(first, sink_term[0], sink_term[1]))
        gate = zc_ref[:, gate0 + tile * BLOCK:gate0 + (tile + 1) * BLOCK].astype(F32)
        o_ref[:, cols] = (num / den * (gate * jax.nn.sigmoid(gate))).astype(BF16)

    pending = [scores(t) for t in range(SWA_AHEAD)]
    unfinished = None
    for tile in range(n_tiles):
        s = pending.pop(0)
        if tile + SWA_AHEAD < n_tiles:
            pending.append(scores(tile + SWA_AHEAD))
        if between is not None:
            between(tile)
        _, vd = kv_of(tile // tiles_per_group)
        res, sink_term = [], []
        for e in range(2):
            head = 2 * tile + e
            se = s[e * BLOCK:(e + 1) * BLOCK]
            window = jnp.where(own, se[:, 2 * BLOCK:], se[:, BLOCK:2 * BLOCK])
            sc = jnp.concatenate([se[:, :BLOCK], window], axis=1) + bias_ref[0, head]
            sink = sink_ref[head] * LOG2E
            m = jnp.maximum(jnp.max(sc, axis=-1, keepdims=True), sink)
            p = jnp.exp2(sc - m)
            pw = p[:, BLOCK:]
            pf = jnp.concatenate([p[:, :BLOCK], jnp.where(own, 0.0, pw), jnp.where(own, pw, 0.0)],
                                 axis=1).astype(BF16)
            res.append(jnp.dot(pf, vd[e], preferred_element_type=F32))
            sink_term.append(jnp.exp2(sink - m))
        if unfinished is not None:
            finish(*unfinished)
        unfinished = (tile, res, sink_term)
    finish(*unfinished)


OUT_SLABS = 4
NORM_ROWS = BLOCK // (SWA_TILES - 2 * OUT_SLABS)


def _swa_deepnorm(n, attn_refs, w_ref, h_prev, gamma_ref, beta_ref, hf_ref, hb_ref, g_refs, y_ref):
    @pl.when(n == 0)
    def _():
        g_refs[1][...] = jnp.zeros(g_refs[1].shape, BF16)

    def step(g_new, g_prev):
        slab = D_MODEL // OUT_SLABS

        def between(tile):
            if tile % 2 == 0 and tile < 2 * OUT_SLABS:
                cols = slice((tile // 2) * slab, (tile // 2 + 1) * slab)
                y_ref[:, cols] = jnp.dot(g_prev[...], w_ref[:, cols], preferred_element_type=F32)
            elif tile >= 2 * OUT_SLABS:
                r = tile - 2 * OUT_SLABS
                rows = slice(r * NORM_ROWS, (r + 1) * NORM_ROWS)
                t = ALPHA * h_prev(rows) + y_ref[rows]
                mu = jnp.mean(t, axis=-1, keepdims=True)
                d = t - mu
                var = jnp.mean(d * d, axis=-1, keepdims=True)
                out = d * lax.rsqrt(var + LN_EPS) * gamma_ref[...] + beta_ref[...]
                hf_ref[rows] = out
                hb_ref[rows] = out.astype(BF16)

        _swa_block(*attn_refs, g_new, between)

    for parity in range(2):
        pl.when(n % 2 == parity)(functools.partial(step, g_refs[parity], g_refs[1 - parity]))


def _swa_deepnorm_first_kernel(sink_ref, zc_ref, zp_ref, zm_ref, bias_ref, w_ref, head_ref, x_ref,
                               gamma_ref, beta_ref, hf_ref, hb_ref, ga_ref, gb_ref, y_ref):
    n = pl.program_id(0)
    in_front = n - 1 < FRONT // BLOCK
    h_prev = lambda rows: jnp.where(in_front, head_ref[rows], x_ref[rows])
    _swa_deepnorm(n, (sink_ref, zc_ref, zp_ref, zm_ref, bias_ref), w_ref, h_prev, gamma_ref, beta_ref,
                  hf_ref, hb_ref, (ga_ref, gb_ref), y_ref)


def _swa_deepnorm_mid_kernel(sink_ref, zc_ref, zp_ref, zm_ref, bias_ref, w_ref, h_ref,
                             gamma_ref, beta_ref, hf_ref, hb_ref, ga_ref, gb_ref, y_ref):
    _swa_deepnorm(pl.program_id(0), (sink_ref, zc_ref, zp_ref, zm_ref, bias_ref), w_ref,
                  lambda rows: h_ref[rows], gamma_ref, beta_ref, hf_ref, hb_ref, (ga_ref, gb_ref), y_ref)


def _swa_out_ln(z, sinks, bias, w_all, layer, gamma, beta, *, h=None, head=None, x=None):
    d = D_MODEL
    kv_tile = A_WIDTH // (2 * A_KV_WIDTH)
    front_blocks = FRONT // BLOCK
    once = pl.Buffered(1)
    cur = lambda n: jnp.minimum(n, N_BLK - 1)
    prev = lambda n: (jnp.maximum(n - 1, 0), 0)
    fixed = lambda n: (0, 0)
    rows = lambda imap: pl.BlockSpec((BLOCK, d), imap)
    attn = [pl.BlockSpec(memory_space=pltpu.SMEM),
            pl.BlockSpec((BLOCK, A_IN), lambda n: (cur(n), 0)),
            pl.BlockSpec((BLOCK, 2 * A_KV_WIDTH), lambda n: (jnp.maximum(cur(n) - 1, 0), kv_tile)),
            pl.BlockSpec((BLOCK, 2 * A_KV_WIDTH), lambda n: (META_BLK, kv_tile)),
            pl.BlockSpec((1, A_Q_HEADS, BLOCK, 2 * BLOCK),
                         lambda n: (jnp.clip(cur(n) - META_BLK, 0, 2), 0, 0, 0)),
            pl.BlockSpec((None, d, d), lambda n: (layer, 0, 0), pipeline_mode=once)]
    if head is not None:
        kern, args = _swa_deepnorm_first_kernel, (head, x)
        resid = [rows(lambda n: (jnp.clip(n - 1, 0, front_blocks - 1), 0)),
                 rows(lambda n: (jnp.maximum(n - 1 - front_blocks, 0), 0))]
    else:
        kern, args, resid = _swa_deepnorm_mid_kernel, (h,), [rows(prev)]
    return pl.pallas_call(
        kern,
        grid=(N_BLK + 1,),
        in_specs=attn + resid + [pl.BlockSpec((1, d), fixed), pl.BlockSpec((1, d), fixed)],
        out_specs=[rows(prev), rows(prev)],
        out_shape=[jax.ShapeDtypeStruct((L_PAD, d), F32), jax.ShapeDtypeStruct((L_PAD, d), BF16)],
        scratch_shapes=[pltpu.VMEM((BLOCK, A_WIDTH), BF16), pltpu.VMEM((BLOCK, A_WIDTH), BF16),
                        pltpu.VMEM((BLOCK, d), F32)],
        name="swa_mixer_out_proj_ln",
        compiler_params=_params(1),
    )(sinks.astype(F32), z, z, z, bias, w_all, *args, gamma, beta)


SOFTPLUS2_DIRECT_MAX = 126.0


def _softplus2(s):
    return jnp.maximum(s, jnp.log2(1.0 + jnp.exp2(jnp.minimum(s, SOFTPLUS2_DIRECT_MAX))))


def _sb_kernel(q_ref, k_ref, v_ref, gate_ref, o_ref, acc_ref, later_ref, dead_ref):
    qi = pl.program_id(1)

    def iotas(width):
        return (lax.broadcasted_iota(jnp.int32, (SB_CH, width), 0),
                lax.broadcasted_iota(jnp.int32, (SB_CH, width), 1))

    suffix = {w: (lax.broadcasted_iota(jnp.int32, (w, w), 0)
                  >= lax.broadcasted_iota(jnp.int32, (w, w), 1)).astype(BF16) for w in (SB_CH, SB_TK)}
    rr, cc = iotas(SB_CH)
    _, cc_wide = iotas(SB_TK)
    pad_col = lambda start: META_ROW0 - start
    n_chunks = SB_TQ // SB_CH

    def visible(kind, start, width):
        if kind is None:
            return None
        if kind == "tri":
            return cc < rr
        if kind == "tri_pad":
            return (cc < rr) & (cc >= pad_col(start))
        assert kind == "pad" and width == SB_TK
        return cc_wide >= pad_col(start)

    def write_gated(rows, lanes, acc):
        gate = gate_ref[rows, lanes].astype(F32)
        o_ref[rows, lanes] = (acc * (gate * jax.nn.sigmoid(gate))).astype(BF16)

    def sweep(spans, first=False):
        items = [(start, width, visible(kind, start, width), c, hh)
                 for (start, width, kind, c) in spans for hh in range(SB_HG)]
        rows = lambda c: slice(c * SB_CH, (c + 1) * SB_CH)
        lanes = lambda hh: slice(hh * B_HEAD_DIM, (hh + 1) * B_HEAD_DIM)
        keys = lambda start, width: pl.ds(start if isinstance(start, int) else pl.multiple_of(start, SB_CH), width)
        later = {(c, hh): (jnp.zeros((SB_CH, B_HEAD_DIM), F32) if first else later_ref[hh, rows(c)])
                 for (_, _, _, c) in spans for hh in range(SB_HG)}
        acc, s, tw, later_before = {}, {}, {}, {}

        def stage_scores(n):
            start, width, _, c, hh = items[n]
            k = k_ref[keys(start, width), lanes(hh)]
            s[n] = lax.dot_general(q_ref[rows(c), lanes(hh)], k, (((1,), (1,)), ((), ())),
                                   preferred_element_type=F32)

        def stage_suffix(n):
            _, width, vis, c, hh = items[n]
            sp = _softplus2(s[n])
            if vis is not None:
                sp = jnp.where(vis, sp, 0.0)
            tw[n] = jnp.dot(sp.astype(BF16), suffix[width], preferred_element_type=F32)
            tot = jnp.sum(sp, axis=-1, keepdims=True)
            later_before[n] = later[(c, hh)]
            later[(c, hh)] = later_before[n] + jnp.broadcast_to(tot, later_before[n].shape)

        def stage_values(n):
            start, width, vis, c, hh = items[n]
            lat = jnp.concatenate([later_before.pop(n)] * (width // B_HEAD_DIM), axis=1)
            p = jnp.exp2(s.pop(n) - tw.pop(n) - lat)
            if vis is not None:
                p = jnp.where(vis, p, 0.0)
            v = v_ref[keys(start, width), lanes(hh)]
            pv = jnp.dot(p.astype(BF16), v, preferred_element_type=F32)
            acc[(c, hh)] = acc[(c, hh)] + pv if (c, hh) in acc else pv

        skew = 2
        for step in range(len(items) + 2 * skew):
            if step < len(items):
                stage_scores(step)
            if 0 <= step - skew < len(items):
                stage_suffix(step - skew)
                if step - skew == len(items) - 1:
                    least = functools.reduce(jnp.minimum, later.values())
                    dead_ref[0] = (jnp.min(least) >= SB_DEAD).astype(jnp.int32)
            if 0 <= step - 2 * skew < len(items):
                stage_values(step - 2 * skew)
        for (c, hh), a in acc.items():
            if first:
                acc_ref[hh, rows(c)] = a
                write_gated(rows(c), lanes(hh), a)
            else:
                acc_ref[hh, rows(c)] += a
            later_ref[hh, rows(c)] = later[(c, hh)]

    assert SB_TQ % SB_TK == 0 and FRONT % SB_TK == 0
    chunks = range(n_chunks)
    own = lambda c: qi * SB_TQ + c * SB_CH

    def to_boundary(start, c):
        full = [(st, SB_TK, None, c) for st in range(start, FRONT - 1, -SB_TK)]
        last = full[-1][0] - SB_TK if full else start
        assert last <= META_ROW0 < last + SB_TK
        return full + [(last, SB_TK, "pad", c)]

    @pl.when(qi == 0)
    def _():
        meta_chunk = META_ROW0 // SB_CH
        acc_ref[:, :meta_chunk * SB_CH] = jnp.zeros((SB_HG, meta_chunk * SB_CH, B_HEAD_DIM), F32)
        o_ref[:meta_chunk * SB_CH] = jnp.zeros((meta_chunk * SB_CH, SB_HG * B_HEAD_DIM), BF16)
        spans = [(meta_chunk * SB_CH, SB_CH, "tri_pad", meta_chunk)]
        for c in chunks[meta_chunk + 1:]:
            spans += [(c * SB_CH, SB_CH, "tri", c)] + to_boundary(c * SB_CH - SB_TK, c)
        sweep(spans, first=True)

    @pl.when(qi >= 1)
    def _():
        sweep([(own(c), SB_CH, "tri", c) for c in chunks]
              + [(own(c) - SB_TK, SB_TK, None, c) for c in chunks], first=True)

    sweep_goes_on = (qi >= 1) & (dead_ref[0] == 0)

    def body(t):
        sweep([(own(c) - t * SB_TK, SB_TK, None, c) for c in chunks])
        return t + 1

    lax.while_loop(lambda t: (own(0) - t * SB_TK >= FRONT) & (dead_ref[0] == 0), body, jnp.int32(2))

    @pl.when((qi >= 1) & (dead_ref[0] == 0))
    def _():
        sweep([span for c in chunks for span in to_boundary(FRONT - SB_TK + c * SB_CH, c)])

    @pl.when(sweep_goes_on)
    def _():
        for hh in range(SB_HG):
            lanes = slice(hh * B_HEAD_DIM, (hh + 1) * B_HEAD_DIM)
            write_gated(slice(None), lanes, acc_ref[hh])


def _stick_breaking(z):
    width = SB_HG * B_HEAD_DIM
    hcols = B_WIDTH // width
    return pl.pallas_call(
        _sb_kernel,
        grid=(B_HEADS // SB_HG, L_PAD // SB_TQ),
        in_specs=[pl.BlockSpec((SB_TQ, width), lambda h, i: (i, h)),
                  pl.BlockSpec((L_PAD, width), lambda h, i: (0, hcols + h)),
                  pl.BlockSpec((L_PAD, width), lambda h, i: (0, 2 * hcols + h)),
                  pl.BlockSpec((SB_TQ, width), lambda h, i: (i, 3 * hcols + h))],
        out_specs=pl.BlockSpec((SB_TQ, width), lambda h, i: (i, h)),
        out_shape=jax.ShapeDtypeStruct((L_PAD, B_WIDTH), BF16),
        scratch_shapes=[pltpu.VMEM((SB_HG, SB_TQ, B_HEAD_DIM), F32),
                        pltpu.VMEM((SB_HG, SB_TQ, B_HEAD_DIM), F32),
                        pltpu.SMEM((1,), jnp.int32)],
        name="stick_breaking_mixer",
        compiler_params=_params(2),
    )(z, z, z, z)


def _q_scale(n_cols, q_cols, scale):
    s = np.ones((1, n_cols), np.float32)
    s[:, :q_cols] = scale
    return jnp.asarray(s)


def _forward_one(x, meta_tokens, rel_bias, w_in_a, sinks_a, w_out_a, w_in_b, w_out_b, ln_g, ln_b):
    x = x.astype(F32)
    head = jnp.concatenate([jnp.zeros((META_ROW0, D_MODEL), F32), meta_tokens.astype(F32)], axis=0)
    bias = _bias_tables(rel_bias)
    scale_a = _q_scale(A_IN, A_WIDTH, A_HEAD_DIM ** -0.5 * LOG2E)
    scale_b = _q_scale(B_IN, B_WIDTH, B_HEAD_DIM ** -0.5 * LOG2E)
    w_in = (w_in_a.astype(F32), w_in_b.astype(F32))
    w_out = (w_out_a.astype(BF16), w_out_b.astype(BF16))
    in_scale = (scale_a, scale_b)
    in_tn = (1536, 1024)
    h = hb = None
    for i in range(DEPTH):
        j, mixer = i // 2, i % 2
        if i == 0:
            z = _in_proj(x, w_in[mixer], j, in_scale[mixer], FRONT, in_tn[mixer], head=head)
        else:
            z = _in_proj(hb, w_in[mixer], j, in_scale[mixer], 1536, in_tn[mixer])
        gamma, beta = ln_g[i][None, :].astype(F32), ln_b[i][None, :].astype(F32)
        if mixer == 0:
            resid = dict(head=head, x=x) if i == 0 else dict(h=h)
            h, hb = _swa_out_ln(z, sinks_a[j], bias, w_out[mixer], j, gamma, beta, **resid)
        elif i == DEPTH - 1:
            return _out_ln(_stick_breaking(z), w_out[mixer], j, gamma, beta, h, last=True)
        else:
            h, hb = _out_ln(_stick_breaking(z), w_out[mixer], j, gamma, beta, h)


def kernel(x, meta_tokens, rel_bias, w_in_a, sinks_a, w_out_a, w_in_b, w_out_b, ln_g, ln_b):
    batch = x.shape[0]
    assert x.shape[1:] == (SEQ, D_MODEL)
    xs = x.reshape(batch * SEQ, D_MODEL)
    outs = [_forward_one(xs[b * SEQ:(b + 1) * SEQ] if batch > 1 else xs, meta_tokens, rel_bias,
                         w_in_a, sinks_a, w_out_a, w_in_b, w_out_b, ln_g, ln_b) for b in range(batch)]
    out = outs[0] if batch == 1 else jnp.concatenate(outs, axis=0)
    return out.reshape(batch, SEQ, D_MODEL).astype(x.dtype)
```

```python
import functools
import math

import jax
import jax.numpy as jnp
import numpy as np
from jax import lax
from jax.experimental import pallas as pl
from jax.experimental.pallas import tpu as pltpu

F32 = jnp.float32
BF16 = jnp.bfloat16

D_MODEL = 2048
SEQ = 16384
DEPTH = 4
N_META = 16
BLOCK = 128
WINDOW = 128
A_HEAD_DIM = 64
A_Q_HEADS = 32
A_KV_HEADS = 4
A_GROUP = 8
A_WIDTH = 2048
A_KV_WIDTH = 256
A_IN = 2 * A_WIDTH + 2 * A_KV_WIDTH
B_HEAD_DIM = 128
B_HEADS = 16
B_WIDTH = 2048
B_IN = 4 * B_WIDTH
N_BUCKETS = 32
MAX_DISTANCE = 128
ALPHA = (2.0 * DEPTH) ** 0.25
LN_EPS = 1e-5
NEG = -1e30
LOG2E = math.log2(math.e)

FRONT = 512
META_ROW0 = FRONT - N_META
L_PAD = FRONT + SEQ
N_BLK = L_PAD // BLOCK
META_BLK = FRONT // BLOCK - 1
META_LOCAL = BLOCK - N_META

VMEM_LIMIT_BYTES = 56 * 1024 * 1024
VMEM_PLAN_FRACTION = 0.9

LN_TM = 512
SB_TQ = 1536
SB_TK = 256
SB_CH = 256
SB_HG = 2
SB_DEAD = 152.0


def _params(n_axes):
    return pltpu.CompilerParams(dimension_semantics=("arbitrary",) * n_axes,
                                vmem_limit_bytes=VMEM_LIMIT_BYTES)


def _cast_weights(w_ref, scale_ref, wb_ref):
    @pl.when(pl.program_id(1) == 0)
    def _():
        wb_ref[...] = (w_ref[...] * scale_ref[...]).astype(BF16)


def _in_proj_kernel(x_ref, w_ref, scale_ref, o_ref, wb_ref):
    _cast_weights(w_ref, scale_ref, wb_ref)
    o_ref[...] = jnp.dot(x_ref[...], wb_ref[...], preferred_element_type=F32).astype(o_ref.dtype)


def _in_proj_first_kernel(head_ref, x_ref, w_ref, scale_ref, o_ref, wb_ref):
    _cast_weights(w_ref, scale_ref, wb_ref)
    x = jnp.where(pl.program_id(1) == 0, head_ref[...], x_ref[...]).astype(BF16)
    o_ref[...] = jnp.dot(x, wb_ref[...], preferred_element_type=F32).astype(o_ref.dtype)


def _in_proj(x, w_all, layer, scale, tm, tn, head=None):
    _, k, n = w_all.shape
    once = pl.Buffered(1)
    planned = (2 * k * tn * 4 + k * tn * 2 + 2 * tm * k * x.dtype.itemsize + 2 * tm * tn * 2 + tm * tn * 4
               + (0 if head is None else tm * k * (4 + 2)))
    w_mode = pl.Buffered(2 if planned <= VMEM_PLAN_FRACTION * VMEM_LIMIT_BYTES else 1)
    w_spec = pl.BlockSpec((None, k, tn), lambda j, i: (layer, 0, j), pipeline_mode=w_mode)
    s_spec = pl.BlockSpec((1, tn), lambda j, i: (0, j))
    if head is None:
        kern = _in_proj_kernel
        in_specs = [pl.BlockSpec((tm, k), lambda j, i: (i, 0)), w_spec, s_spec]
        args = (x, w_all, scale)
    else:
        assert tm == FRONT
        kern = _in_proj_first_kernel
        in_specs = [pl.BlockSpec((tm, k), lambda j, i: (0, 0), pipeline_mode=once),
                    pl.BlockSpec((tm, k), lambda j, i: (jnp.maximum(i - 1, 0), 0)), w_spec, s_spec]
        args = (head, x, w_all, scale)
    return pl.pallas_call(
        kern,
        grid=(n // tn, L_PAD // tm),
        in_specs=in_specs,
        out_specs=pl.BlockSpec((tm, tn), lambda j, i: (i, j)),
        out_shape=jax.ShapeDtypeStruct((L_PAD, n), BF16),
        scratch_shapes=[pltpu.VMEM((k, tn), BF16)],
        name="in_proj",
        compiler_params=_params(2),
    )(*args)


LN_TILES = L_PAD // LN_TM


LN_CHUNK = 128


def _project_and_norm(g_ref, w_ref, h_prev, gamma_ref, beta_ref, y_refs, out_refs):
    i = pl.program_id(0)

    @pl.when(i == 0)
    def _():
        y_refs[1][...] = jnp.zeros(y_refs[1].shape, F32)

    def step(y_new, y_prev):
        y_new[...] = jnp.dot(g_ref[...], w_ref[...], preferred_element_type=F32)
        for c in range(LN_TM // LN_CHUNK):
            rows = slice(c * LN_CHUNK, (c + 1) * LN_CHUNK)
            t = ALPHA * h_prev(rows) + y_prev[rows]
            mu = jnp.mean(t, axis=-1, keepdims=True)
            d = t - mu
            var = jnp.mean(d * d, axis=-1, keepdims=True)
            out = d * lax.rsqrt(var + LN_EPS) * gamma_ref[...] + beta_ref[...]
            for ref in out_refs:
                ref[rows] = out.astype(ref.dtype)

    for parity in range(2):
        pl.when(i % 2 == parity)(functools.partial(step, y_refs[parity], y_refs[1 - parity]))


def _out_ln_mid_kernel(g_ref, w_ref, h_ref, gamma_ref, beta_ref, hf_ref, hb_ref, ya_ref, yb_ref):
    _project_and_norm(g_ref, w_ref, lambda rows: h_ref[rows], gamma_ref, beta_ref, (ya_ref, yb_ref),
                      (hf_ref, hb_ref))


def _out_ln_last_kernel(g_ref, w_ref, h_ref, gamma_ref, beta_ref, o_ref, ya_ref, yb_ref):
    _project_and_norm(g_ref, w_ref, lambda rows: h_ref[rows], gamma_ref, beta_ref, (ya_ref, yb_ref), (o_ref,))


def _out_ln(g, w_all, layer, gamma, beta, h, last=False):
    d = D_MODEL
    assert FRONT == LN_TM
    fixed = lambda i: (0, 0)
    prev = lambda i: (jnp.maximum(i - 1, 0), 0)
    prev_tok = lambda i: (jnp.maximum(i - 2, 0), 0)
    tile = lambda imap: pl.BlockSpec((LN_TM, d), imap)
    if last:
        kern = _out_ln_last_kernel
        outs = dict(out_specs=tile(prev_tok), out_shape=jax.ShapeDtypeStruct((SEQ, d), F32))
    else:
        kern = _out_ln_mid_kernel
        outs = dict(out_specs=[tile(prev), tile(prev)],
                    out_shape=[jax.ShapeDtypeStruct((L_PAD, d), F32), jax.ShapeDtypeStruct((L_PAD, d), BF16)])
    return pl.pallas_call(
        kern,
        grid=(LN_TILES + 1,),
        in_specs=[tile(lambda i: (jnp.minimum(i, LN_TILES - 1), 0)),
                  pl.BlockSpec((None, d, d), lambda i: (layer, 0, 0), pipeline_mode=pl.Buffered(1)),
                  tile(prev), pl.BlockSpec((1, d), fixed), pl.BlockSpec((1, d), fixed)],
        scratch_shapes=[pltpu.VMEM((LN_TM, d), F32), pltpu.VMEM((LN_TM, d), F32)],
        name="out_proj_ln",
        compiler_params=_params(1),
        **outs,
    )(g, w_all, h, gamma, beta)


def _t5_bucket_np(dist):
    max_exact = N_BUCKETS // 2
    d = np.maximum(dist, 0)
    df = np.maximum(d, 1).astype(np.float32)
    large = max_exact + (np.log(df / np.float32(max_exact)) / np.float32(math.log(MAX_DISTANCE / max_exact))
                         * np.float32(N_BUCKETS - max_exact)).astype(np.int32)
    large = np.minimum(large, N_BUCKETS - 1)
    return np.where(d < max_exact, d, large).astype(np.int32)


def _bucket_maps():
    masked = N_BUCKETS
    r = np.arange(BLOCK)[:, None]
    c = np.arange(BLOCK)[None, :]
    out = np.full((3, BLOCK, 2 * BLOCK), masked, np.int32)
    ok = (r >= META_LOCAL) & (c >= META_LOCAL) & (r - c >= 0)
    out[0, :, :BLOCK] = np.where(ok, _t5_bucket_np(r - c), masked)
    own = _t5_bucket_np(r - c)
    prev = _t5_bucket_np(r + BLOCK - c)
    out[1, :, :BLOCK] = np.where(c >= META_LOCAL, _t5_bucket_np(r + BLOCK - c), masked)
    out[1, :, BLOCK:] = np.where(c <= r, own, masked)
    far = _t5_bucket_np(np.array(2 * BLOCK + 1))
    out[2, :, :BLOCK] = np.where(c >= META_LOCAL, far, masked) + 0 * r
    out[2, :, BLOCK:] = np.where(c <= r, own, prev)
    return out


def _bias_kernel(rel_ref, map_ref, o_ref):
    bmap = map_ref[0]

    def one_head(h, carry):
        acc = jnp.full(bmap.shape, NEG, F32)
        for b in range(N_BUCKETS):
            acc = jnp.where(bmap == b, rel_ref[b, h] * LOG2E, acc)
        o_ref[0, h] = acc
        return carry

    lax.fori_loop(0, A_Q_HEADS, one_head, 0)


def _bias_tables(rel_bias):
    maps = jnp.asarray(_bucket_maps())
    return pl.pallas_call(
        _bias_kernel,
        grid=(3,),
        in_specs=[pl.BlockSpec(memory_space=pltpu.SMEM),
                  pl.BlockSpec((1, BLOCK, 2 * BLOCK), lambda c: (c, 0, 0))],
        out_specs=pl.BlockSpec((1, A_Q_HEADS, BLOCK, 2 * BLOCK), lambda c: (c, 0, 0, 0)),
        out_shape=jax.ShapeDtypeStruct((3, A_Q_HEADS, BLOCK, 2 * BLOCK), F32),
        name="bias_tables",
        compiler_params=_params(1),
    )(rel_bias.astype(F32), maps)


SWA_TILES = A_Q_HEADS // 2
SWA_AHEAD = 4


def _swa_block(sink_ref, zc_ref, zp_ref, zm_ref, bias_ref, o_ref, between=None):
    kv0 = A_WIDTH
    kcat = jnp.concatenate([zm_ref[:, :A_KV_WIDTH], zp_ref[:, :A_KV_WIDTH],
                            zc_ref[:, kv0:kv0 + A_KV_WIDTH]], axis=0)
    vcat = jnp.concatenate([zm_ref[:, A_KV_WIDTH:], zp_ref[:, A_KV_WIDTH:],
                            zc_ref[:, kv0 + A_KV_WIDTH:kv0 + 2 * A_KV_WIDTH]], axis=0)
    gate0 = kv0 + 2 * A_KV_WIDTH
    lane_k = lax.broadcasted_iota(jnp.int32, (3 * BLOCK, BLOCK), 1) < A_HEAD_DIM
    first = lax.broadcasted_iota(jnp.int32, (BLOCK, BLOCK), 1) < A_HEAD_DIM
    own = (lax.broadcasted_iota(jnp.int32, (BLOCK, BLOCK), 0)
           >= lax.broadcasted_iota(jnp.int32, (BLOCK, BLOCK), 1))
    tiles_per_group = A_GROUP // 2
    n_tiles = SWA_TILES
    kv_cache = {}

    def kv_of(g):
        if g not in kv_cache:
            pair = slice((g // 2) * BLOCK, (g // 2 + 1) * BLOCK)
            k2 = kcat[:, pair].astype(F32)
            v2 = vcat[:, pair].astype(F32)
            k2r = pltpu.roll(k2, A_HEAD_DIM, axis=1)
            v2r = pltpu.roll(v2, A_HEAD_DIM, axis=1)
            if g % 2 == 0:
                kd, v_lo, v_hi = jnp.where(lane_k, k2, k2r), v2, v2r
            else:
                kd, v_lo, v_hi = jnp.where(lane_k, k2r, k2), v2r, v2
            vd = [jnp.where(lane_k, v_lo, 1.0).astype(BF16), jnp.where(lane_k, 1.0, v_hi).astype(BF16)]
            kv_cache[g] = (kd.astype(BF16), vd)
        return kv_cache[g]

    def scores(tile):
        qt = zc_ref[:, tile * BLOCK:(tile + 1) * BLOCK]
        zero = jnp.zeros_like(qt)
        qq = jnp.concatenate([jnp.where(first, qt, zero), jnp.where(first, zero, qt)], axis=0)
        kd, _ = kv_of(tile // tiles_per_group)
        return lax.dot_general(qq, kd, (((1,), (1,)), ((), ())), preferred_element_type=F32)

    def finish(tile, res, sink_term):
        cols = slice(tile * BLOCK, (tile + 1) * BLOCK)
        num = jnp.where(first, res[0], res[1])
        den = (pltpu.roll(jnp.where(first, res[1], res[0]), A_HEAD_DIM, axis=1)
               + jnp.where(first, sink_term[0], sink_term[1]))
        gate = zc_ref[:, gate0 + tile * BLOCK:gate0 + (tile + 1) * BLOCK].astype(F32)
        o_ref[:, cols] = (num / den * (gate * jax.nn.sigmoid(gate))).astype(BF16)

    pending = [scores(t) for t in range(SWA_AHEAD)]
    unfinished = None
    for tile in range(n_tiles):
        s = pending.pop(0)
        if tile + SWA_AHEAD < n_tiles:
            pending.append(scores(tile + SWA_AHEAD))
        if between is not None:
            between(tile)
        _, vd = kv_of(tile // tiles_per_group)
        res, sink_term = [], []
        for e in range(2):
            head = 2 * tile + e
            se = s[e * BLOCK:(e + 1) * BLOCK]
            window = jnp.where(own, se[:, 2 * BLOCK:], se[:, BLOCK:2 * BLOCK])
            sc = jnp.concatenate([se[:, :BLOCK], window], axis=1) + bias_ref[0, head]
            sink = sink_ref[head] * LOG2E
            m = jnp.maximum(jnp.max(sc, axis=-1, keepdims=True), sink)
            p = jnp.exp2(sc - m)
            pw = p[:, BLOCK:]
            pf = jnp.concatenate([p[:, :BLOCK], jnp.where(own, 0.0, pw), jnp.where(own, pw, 0.0)],
                                 axis=1).astype(BF16)
            res.append(jnp.dot(pf, vd[e], preferred_element_type=F32))
            sink_term.append(jnp.exp2(sink - m))
        if unfinished is not None:
            finish(*unfinished)
        unfinished = (tile, res, sink_term)
    finish(*unfinished)


OUT_SLABS = 4
NORM_ROWS = BLOCK // (SWA_TILES - 2 * OUT_SLABS)


def _swa_deepnorm(n, attn_refs, w_ref, h_prev, gamma_ref, beta_ref, hf_ref, hb_ref, g_refs, y_ref):
    @pl.when(n == 0)
    def _():
        g_refs[1][...] = jnp.zeros(g_refs[1].shape, BF16)

    def step(g_new, g_prev):
        slab = D_MODEL // OUT_SLABS

        def between(tile):
            if tile % 2 == 0 and tile < 2 * OUT_SLABS:
                cols = slice((tile // 2) * slab, (tile // 2 + 1) * slab)
                y_ref[:, cols] = jnp.dot(g_prev[...], w_ref[:, cols], preferred_element_type=F32)
            elif tile >= 2 * OUT_SLABS:
                r = tile - 2 * OUT_SLABS
                rows = slice(r * NORM_ROWS, (r + 1) * NORM_ROWS)
                t = ALPHA * h_prev(rows) + y_ref[rows]
                mu = jnp.mean(t, axis=-1, keepdims=True)
                d = t - mu
                var = jnp.mean(d * d, axis=-1, keepdims=True)
                out = d * lax.rsqrt(var + LN_EPS) * gamma_ref[...] + beta_ref[...]
                hf_ref[rows] = out
                hb_ref[rows] = out.astype(BF16)

        _swa_block(*attn_refs, g_new, between)

    for parity in range(2):
        pl.when(n % 2 == parity)(functools.partial(step, g_refs[parity], g_refs[1 - parity]))


def _swa_deepnorm_first_kernel(sink_ref, zc_ref, zp_ref, zm_ref, bias_ref, w_ref, head_ref, x_ref,
                               gamma_ref, beta_ref, hf_ref, hb_ref, ga_ref, gb_ref, y_ref):
    n = pl.program_id(0)
    in_front = n - 1 < FRONT // BLOCK
    h_prev = lambda rows: jnp.where(in_front, head_ref[rows], x_ref[rows])
    _swa_deepnorm(n, (sink_ref, zc_ref, zp_ref, zm_ref, bias_ref), w_ref, h_prev, gamma_ref, beta_ref,
                  hf_ref, hb_ref, (ga_ref, gb_ref), y_ref)


def _swa_deepnorm_mid_kernel(sink_ref, zc_ref, zp_ref, zm_ref, bias_ref, w_ref, h_ref,
                             gamma_ref, beta_ref, hf_ref, hb_ref, ga_ref, gb_ref, y_ref):
    _swa_deepnorm(pl.program_id(0), (sink_ref, zc_ref, zp_ref, zm_ref, bias_ref), w_ref,
                  lambda rows: h_ref[rows], gamma_ref, beta_ref, hf_ref, hb_ref, (ga_ref, gb_ref), y_ref)


def _swa_out_ln(z, sinks, bias, w_all, layer, gamma, beta, *, h=None, head=None, x=None):
    d = D_MODEL
    kv_tile = A_WIDTH // (2 * A_KV_WIDTH)
    front_blocks = FRONT // BLOCK
    once = pl.Buffered(1)
    cur = lambda n: jnp.minimum(n, N_BLK - 1)
    prev = lambda n: (jnp.maximum(n - 1, 0), 0)
    fixed = lambda n: (0, 0)
    rows = lambda imap: pl.BlockSpec((BLOCK, d), imap)
    attn = [pl.BlockSpec(memory_space=pltpu.SMEM),
            pl.BlockSpec((BLOCK, A_IN), lambda n: (cur(n), 0)),
            pl.BlockSpec((BLOCK, 2 * A_KV_WIDTH), lambda n: (jnp.maximum(cur(n) - 1, 0), kv_tile)),
            pl.BlockSpec((BLOCK, 2 * A_KV_WIDTH), lambda n: (META_BLK, kv_tile)),
            pl.BlockSpec((1, A_Q_HEADS, BLOCK, 2 * BLOCK),
                         lambda n: (jnp.clip(cur(n) - META_BLK, 0, 2), 0, 0, 0)),
            pl.BlockSpec((None, d, d), lambda n: (layer, 0, 0), pipeline_mode=once)]
    if head is not None:
        kern, args = _swa_deepnorm_first_kernel, (head, x)
        resid = [rows(lambda n: (jnp.clip(n - 1, 0, front_blocks - 1), 0)),
                 rows(lambda n: (jnp.maximum(n - 1 - front_blocks, 0), 0))]
    else:
        kern, args, resid = _swa_deepnorm_mid_kernel, (h,), [rows(prev)]
    return pl.pallas_call(
        kern,
        grid=(N_BLK + 1,),
        in_specs=attn + resid + [pl.BlockSpec((1, d), fixed), pl.BlockSpec((1, d), fixed)],
        out_specs=[rows(prev), rows(prev)],
        out_shape=[jax.ShapeDtypeStruct((L_PAD, d), F32), jax.ShapeDtypeStruct((L_PAD, d), BF16)],
        scratch_shapes=[pltpu.VMEM((BLOCK, A_WIDTH), BF16), pltpu.VMEM((BLOCK, A_WIDTH), BF16),
                        pltpu.VMEM((BLOCK, d), F32)],
        name="swa_mixer_out_proj_ln",
        compiler_params=_params(1),
    )(sinks.astype(F32), z, z, z, bias, w_all, *args, gamma, beta)


SOFTPLUS2_DIRECT_MAX = 126.0


def _softplus2(s):
    return jnp.maximum(s, jnp.log2(1.0 + jnp.exp2(jnp.minimum(s, SOFTPLUS2_DIRECT_MAX))))


def _sb_kernel(q_ref, k_ref, v_ref, gate_ref, o_ref, acc_ref, later_ref, dead_ref):
    qi = pl.program_id(1)

    def iotas(width):
        return (lax.broadcasted_iota(jnp.int32, (SB_CH, width), 0),
                lax.broadcasted_iota(jnp.int32, (SB_CH, width), 1))

    suffix = {w: (lax.broadcasted_iota(jnp.int32, (w, w), 0)
                  >= lax.broadcasted_iota(jnp.int32, (w, w), 1)).astype(BF16) for w in (SB_CH, SB_TK)}
    rr, cc = iotas(SB_CH)
    _, cc_wide = iotas(SB_TK)
    pad_col = lambda start: META_ROW0 - start
    n_chunks = SB_TQ // SB_CH

    def visible(kind, start, width):
        if kind is None:
            return None
        if kind == "tri":
            return cc < rr
        if kind == "tri_pad":
            return (cc < rr) & (cc >= pad_col(start))
        assert kind == "pad" and width == SB_TK
        return cc_wide >= pad_col(start)

    def write_gated(rows, lanes, acc):
        gate = gate_ref[rows, lanes].astype(F32)
        o_ref[rows, lanes] = (acc * (gate * jax.nn.sigmoid(gate))).astype(BF16)

    def sweep(spans, first=False):
        items = [(start, width, visible(kind, start, width), c, hh)
                 for (start, width, kind, c) in spans for hh in range(SB_HG)]
        rows = lambda c: slice(c * SB_CH, (c + 1) * SB_CH)
        lanes = lambda hh: slice(hh * B_HEAD_DIM, (hh + 1) * B_HEAD_DIM)
        keys = lambda start, width: pl.ds(start if isinstance(start, int) else pl.multiple_of(start, SB_CH), width)
        later = {(c, hh): (jnp.zeros((SB_CH, B_HEAD_DIM), F32) if first else later_ref[hh, rows(c)])
                 for (_, _, _, c) in spans for hh in range(SB_HG)}
        acc, s, tw, later_before = {}, {}, {}, {}

        def stage_scores(n):
            start, width, _, c, hh = items[n]
            k = k_ref[keys(start, width), lanes(hh)]
            s[n] = lax.dot_general(q_ref[rows(c), lanes(hh)], k, (((1,), (1,)), ((), ())),
                                   preferred_element_type=F32)

        def stage_suffix(n):
            _, width, vis, c, hh = items[n]
            sp = _softplus2(s[n])
            if vis is not None:
                sp = jnp.where(vis, sp, 0.0)
            tw[n] = jnp.dot(sp.astype(BF16), suffix[width], preferred_element_type=F32)
            tot = jnp.sum(sp, axis=-1, keepdims=True)
            later_before[n] = later[(c, hh)]
            later[(c, hh)] = later_before[n] + jnp.broadcast_to(tot, later_before[n].shape)

        def stage_values(n):
            start, width, vis, c, hh = items[n]
            lat = jnp.concatenate([later_before.pop(n)] * (width // B_HEAD_DIM), axis=1)
            p = jnp.exp2(s.pop(n) - tw.pop(n) - lat)
            if vis is not None:
                p = jnp.where(vis, p, 0.0)
            v = v_ref[keys(start, width), lanes(hh)]
            pv = jnp.dot(p.astype(BF16), v, preferred_element_type=F32)
            acc[(c, hh)] = acc[(c, hh)] + pv if (c, hh) in acc else pv

        skew = 2
        for step in range(len(items) + 2 * skew):
            if step < len(items):
                stage_scores(step)
            if 0 <= step - skew < len(items):
                stage_suffix(step - skew)
                if step - skew == len(items) - 1:
                    least = functools.reduce(jnp.minimum, later.values())
                    dead_ref[0] = (jnp.min(least) >= SB_DEAD).astype(jnp.int32)
            if 0 <= step - 2 * skew < len(items):
                stage_values(step - 2 * skew)
        for (c, hh), a in acc.items():
            if first:
                acc_ref[hh, rows(c)] = a
                write_gated(rows(c), lanes(hh), a)
            else:
                acc_ref[hh, rows(c)] += a
            later_ref[hh, rows(c)] = later[(c, hh)]

    assert SB_TQ % SB_TK == 0 and FRONT % SB_TK == 0
    chunks = range(n_chunks)
    own = lambda c: qi * SB_TQ + c * SB_CH

    def to_boundary(start, c):
        full = [(st, SB_TK, None, c) for st in range(start, FRONT - 1, -SB_TK)]
        last = full[-1][0] - SB_TK if full else start
        assert last <= META_ROW0 < last + SB_TK
        return full + [(last, SB_TK, "pad", c)]

    @pl.when(qi == 0)
    def _():
        meta_chunk = META_ROW0 // SB_CH
        acc_ref[:, :meta_chunk * SB_CH] = jnp.zeros((SB_HG, meta_chunk * SB_CH, B_HEAD_DIM), F32)
        o_ref[:meta_chunk * SB_CH] = jnp.zeros((meta_chunk * SB_CH, SB_HG * B_HEAD_DIM), BF16)
        spans = [(meta_chunk * SB_CH, SB_CH, "tri_pad", meta_chunk)]
        for c in chunks[meta_chunk + 1:]:
            spans += [(c * SB_CH, SB_CH, "tri", c)] + to_boundary(c * SB_CH - SB_TK, c)
        sweep(spans, first=True)

    @pl.when(qi >= 1)
    def _():
        sweep([(own(c), SB_CH, "tri", c) for c in chunks]
              + [(own(c) - SB_TK, SB_TK, None, c) for c in chunks], first=True)

    sweep_goes_on = (qi >= 1) & (dead_ref[0] == 0)

    def body(t):
        sweep([(own(c) - t * SB_TK, SB_TK, None, c) for c in chunks])
        return t + 1

    lax.while_loop(lambda t: (own(0) - t * SB_TK >= FRONT) & (dead_ref[0] == 0), body, jnp.int32(2))

    @pl.when((qi >= 1) & (dead_ref[0] == 0))
    def _():
        sweep([span for c in chunks for span in to_boundary(FRONT - SB_TK + c * SB_CH, c)])

    @pl.when(sweep_goes_on)
    def _():
        for hh in range(SB_HG):
            lanes = slice(hh * B_HEAD_DIM, (hh + 1) * B_HEAD_DIM)
            write_gated(slice(None), lanes, acc_ref[hh])


def _stick_breaking(z):
    width = SB_HG * B_HEAD_DIM
    hcols = B_WIDTH // width
    return pl.pallas_call(
        _sb_kernel,
        grid=(B_HEADS // SB_HG, L_PAD // SB_TQ),
        in_specs=[pl.BlockSpec((SB_TQ, width), lambda h, i: (i, h)),
                  pl.BlockSpec((L_PAD, width), lambda h, i: (0, hcols + h)),
                  pl.BlockSpec((L_PAD, width), lambda h, i: (0, 2 * hcols + h)),
                  pl.BlockSpec((SB_TQ, width), lambda h, i: (i, 3 * hcols + h))],
        out_specs=pl.BlockSpec((SB_TQ, width), lambda h, i: (i, h)),
        out_shape=jax.ShapeDtypeStruct((L_PAD, B_WIDTH), BF16),
        scratch_shapes=[pltpu.VMEM((SB_HG, SB_TQ, B_HEAD_DIM), F32),
                        pltpu.VMEM((SB_HG, SB_TQ, B_HEAD_DIM), F32),
                        pltpu.SMEM((1,), jnp.int32)],
        name="stick_breaking_mixer",
        compiler_params=_params(2),
    )(z, z, z, z)


def _q_scale(n_cols, q_cols, scale):
    s = np.ones((1, n_cols), np.float32)
    s[:, :q_cols] = scale
    return jnp.asarray(s)


def _forward_one(x, meta_tokens, rel_bias, w_in_a, sinks_a, w_out_a, w_in_b, w_out_b, ln_g, ln_b):
    x = x.astype(F32)
    head = jnp.concatenate([jnp.zeros((META_ROW0, D_MODEL), F32), meta_tokens.astype(F32)], axis=0)
    bias = _bias_tables(rel_bias)
    scale_a = _q_scale(A_IN, A_WIDTH, A_HEAD_DIM ** -0.5 * LOG2E)
    scale_b = _q_scale(B_IN, B_WIDTH, B_HEAD_DIM ** -0.5 * LOG2E)
    w_in = (w_in_a.astype(F32), w_in_b.astype(F32))
    w_out = (w_out_a.astype(BF16), w_out_b.astype(BF16))
    in_scale = (scale_a, scale_b)
    in_tn = (1536, 1024)
    h = hb = None
    for i in range(DEPTH):
        j, mixer = i // 2, i % 2
        if i == 0:
            z = _in_proj(x, w_in[mixer], j, in_scale[mixer], FRONT, in_tn[mixer], head=head)
        else:
            z = _in_proj(hb, w_in[mixer], j, in_scale[mixer], 1536, in_tn[mixer])
        gamma, beta = ln_g[i][None, :].astype(F32), ln_b[i][None, :].astype(F32)
        if mixer == 0:
            resid = dict(head=head, x=x) if i == 0 else dict(h=h)
            h, hb = _swa_out_ln(z, sinks_a[j], bias, w_out[mixer], j, gamma, beta, **resid)
        elif i == DEPTH - 1:
            return _out_ln(_stick_breaking(z), w_out[mixer], j, gamma, beta, h, last=True)
        else:
            h, hb = _out_ln(_stick_breaking(z), w_out[mixer], j, gamma, beta, h)


def kernel(x, meta_tokens, rel_bias, w_in_a, sinks_a, w_out_a, w_in_b, w_out_b, ln_g, ln_b):
    batch = x.shape[0]
    assert x.shape[1:] == (SEQ, D_MODEL)
    xs = x.reshape(batch * SEQ, D_MODEL)
    outs = [_forward_one(xs[b * SEQ:(b + 1) * SEQ] if batch > 1 else xs, meta_tokens, rel_bias,
                         w_in_a, sinks_a, w_out_a, w_in_b, w_out_b, ln_g, ln_b) for b in range(batch)]
    out = outs[0] if batch == 1 else jnp.concatenate(outs, axis=0)
    return out.reshape(batch, SEQ, D_MODEL).astype(x.dtype)
```

```python
import functools
import math

import jax
import jax.numpy as jnp
import numpy as np
from jax import lax
from jax.experimental import pallas as pl
from jax.experimental.pallas import tpu as pltpu

F32 = jnp.float32
BF16 = jnp.bfloat16

D_MODEL = 2048
SEQ = 16384
DEPTH = 4
N_META = 16
BLOCK = 128
WINDOW = 128
A_HEAD_DIM = 64
A_Q_HEADS = 32
A_KV_HEADS = 4
A_GROUP = 8
A_WIDTH = 2048
A_KV_WIDTH = 256
A_IN = 2 * A_WIDTH + 2 * A_KV_WIDTH
B_HEAD_DIM = 128
B_HEADS = 16
B_WIDTH = 2048
B_IN = 4 * B_WIDTH
N_BUCKETS = 32
MAX_DISTANCE = 128
ALPHA = (2.0 * DEPTH) ** 0.25
LN_EPS = 1e-5
NEG = -1e30
LOG2E = math.log2(math.e)

FRONT = 512
META_ROW0 = FRONT - N_META
L_PAD = FRONT + SEQ
N_BLK = L_PAD // BLOCK
META_BLK = FRONT // BLOCK - 1
META_LOCAL = BLOCK - N_META

VMEM_LIMIT_BYTES = 56 * 1024 * 1024
VMEM_PLAN_FRACTION = 0.9

LN_TM = 512
SB_TQ = 1536
SB_TK = 256
SB_CH = 256
SB_HG = 2
SB_DEAD = 152.0


def _params(n_axes):
    return pltpu.CompilerParams(dimension_semantics=("arbitrary",) * n_axes,
                                vmem_limit_bytes=VMEM_LIMIT_BYTES)


def _cast_weights(w_ref, scale_ref, wb_ref):
    @pl.when(pl.program_id(1) == 0)
    def _():
        wb_ref[...] = (w_ref[...] * scale_ref[...]).astype(BF16)


def _in_proj_kernel(x_ref, w_ref, scale_ref, o_ref, wb_ref):
    _cast_weights(w_ref, scale_ref, wb_ref)
    o_ref[...] = jnp.dot(x_ref[...], wb_ref[...], preferred_element_type=F32).astype(o_ref.dtype)


def _in_proj_first_kernel(head_ref, x_ref, w_ref, scale_ref, o_ref, wb_ref):
    _cast_weights(w_ref, scale_ref, wb_ref)
    x = jnp.where(pl.program_id(1) == 0, head_ref[...], x_ref[...]).astype(BF16)
    o_ref[...] = jnp.dot(x, wb_ref[...], preferred_element_type=F32).astype(o_ref.dtype)


def _in_proj(x, w_all, layer, scale, tm, tn, head=None):
    _, k, n = w_all.shape
    once = pl.Buffered(1)
    planned = (2 * k * tn * 4 + k * tn * 2 + 2 * tm * k * x.dtype.itemsize + 2 * tm * tn * 2 + tm * tn * 4
               + (0 if head is None else tm * k * (4 + 2)))
    w_mode = pl.Buffered(2 if planned <= VMEM_PLAN_FRACTION * VMEM_LIMIT_BYTES else 1)
    w_spec = pl.BlockSpec((None, k, tn), lambda j, i: (layer, 0, j), pipeline_mode=w_mode)
    s_spec = pl.BlockSpec((1, tn), lambda j, i: (0, j))
    if head is None:
        kern = _in_proj_kernel
        in_specs = [pl.BlockSpec((tm, k), lambda j, i: (i, 0)), w_spec, s_spec]
        args = (x, w_all, scale)
    else:
        assert tm == FRONT
        kern = _in_proj_first_kernel
        in_specs = [pl.BlockSpec((tm, k), lambda j, i: (0, 0), pipeline_mode=once),
                    pl.BlockSpec((tm, k), lambda j, i: (jnp.maximum(i - 1, 0), 0)), w_spec, s_spec]
        args = (head, x, w_all, scale)
    return pl.pallas_call(
        kern,
        grid=(n // tn, L_PAD // tm),
        in_specs=in_specs,
        out_specs=pl.BlockSpec((tm, tn), lambda j, i: (i, j)),
        out_shape=jax.ShapeDtypeStruct((L_PAD, n), BF16),
        scratch_shapes=[pltpu.VMEM((k, tn), BF16)],
        name="in_proj",
        compiler_params=_params(2),
    )(*args)


LN_TILES = L_PAD // LN_TM


LN_CHUNK = 128


def _project_and_norm(g_ref, w_ref, h_prev, gamma_ref, beta_ref, y_refs, out_refs):
    i = pl.program_id(0)

    @pl.when(i == 0)
    def _():
        y_refs[1][...] = jnp.zeros(y_refs[1].shape, F32)

    def step(y_new, y_prev):
        y_new[...] = jnp.dot(g_ref[...], w_ref[...], preferred_element_type=F32)
        for c in range(LN_TM // LN_CHUNK):
            rows = slice(c * LN_CHUNK, (c + 1) * LN_CHUNK)
            t = ALPHA * h_prev(rows) + y_prev[rows]
            mu = jnp.mean(t, axis=-1, keepdims=True)
            d = t - mu
            var = jnp.mean(d * d, axis=-1, keepdims=True)
            out = d * lax.rsqrt(var + LN_EPS) * gamma_ref[...] + beta_ref[...]
            for ref in out_refs:
                ref[rows] = out.astype(ref.dtype)

    for parity in range(2):
        pl.when(i % 2 == parity)(functools.partial(step, y_refs[parity], y_refs[1 - parity]))


def _out_ln_mid_kernel(g_ref, w_ref, h_ref, gamma_ref, beta_ref, hf_ref, hb_ref, ya_ref, yb_ref):
    _project_and_norm(g_ref, w_ref, lambda rows: h_ref[rows], gamma_ref, beta_ref, (ya_ref, yb_ref),
                      (hf_ref, hb_ref))


def _out_ln_last_kernel(g_ref, w_ref, h_ref, gamma_ref, beta_ref, o_ref, ya_ref, yb_ref):
    _project_and_norm(g_ref, w_ref, lambda rows: h_ref[rows], gamma_ref, beta_ref, (ya_ref, yb_ref), (o_ref,))


def _out_ln(g, w_all, layer, gamma, beta, h, last=False):
    d = D_MODEL
    assert FRONT == LN_TM
    fixed = lambda i: (0, 0)
    prev = lambda i: (jnp.maximum(i - 1, 0), 0)
    prev_tok = lambda i: (jnp.maximum(i - 2, 0), 0)
    tile = lambda imap: pl.BlockSpec((LN_TM, d), imap)
    if last:
        kern = _out_ln_last_kernel
        outs = dict(out_specs=tile(prev_tok), out_shape=jax.ShapeDtypeStruct((SEQ, d), F32))
    else:
        kern = _out_ln_mid_kernel
        outs = dict(out_specs=[tile(prev), tile(prev)],
                    out_shape=[jax.ShapeDtypeStruct((L_PAD, d), F32), jax.ShapeDtypeStruct((L_PAD, d), BF16)])
    return pl.pallas_call(
        kern,
        grid=(LN_TILES + 1,),
        in_specs=[tile(lambda i: (jnp.minimum(i, LN_TILES - 1), 0)),
                  pl.BlockSpec((None, d, d), lambda i: (layer, 0, 0), pipeline_mode=pl.Buffered(1)),
                  tile(prev), pl.BlockSpec((1, d), fixed), pl.BlockSpec((1, d), fixed)],
        scratch_shapes=[pltpu.VMEM((LN_TM, d), F32), pltpu.VMEM((LN_TM, d), F32)],
        name="out_proj_ln",
        compiler_params=_params(1),
        **outs,
    )(g, w_all, h, gamma, beta)


def _t5_bucket_np(dist):
    max_exact = N_BUCKETS // 2
    d = np.maximum(dist, 0)
    df = np.maximum(d, 1).astype(np.float32)
    large = max_exact + (np.log(df / np.float32(max_exact)) / np.float32(math.log(MAX_DISTANCE / max_exact))
                         * np.float32(N_BUCKETS - max_exact)).astype(np.int32)
    large = np.minimum(large, N_BUCKETS - 1)
    return np.where(d < max_exact, d, large).astype(np.int32)


def _bucket_maps():
    masked = N_BUCKETS
    r = np.arange(BLOCK)[:, None]
    c = np.arange(BLOCK)[None, :]
    out = np.full((3, BLOCK, 2 * BLOCK), masked, np.int32)
    ok = (r >= META_LOCAL) & (c >= META_LOCAL) & (r - c >= 0)
    out[0, :, :BLOCK] = np.where(ok, _t5_bucket_np(r - c), masked)
    own = _t5_bucket_np(r - c)
    prev = _t5_bucket_np(r + BLOCK - c)
    out[1, :, :BLOCK] = np.where(c >= META_LOCAL, _t5_bucket_np(r + BLOCK - c), masked)
    out[1, :, BLOCK:] = np.where(c <= r, own, masked)
    far = _t5_bucket_np(np.array(2 * BLOCK + 1))
    out[2, :, :BLOCK] = np.where(c >= META_LOCAL, far, masked) + 0 * r
    out[2, :, BLOCK:] = np.where(c <= r, own, prev)
    return out


def _bias_kernel(rel_ref, map_ref, o_ref):
    bmap = map_ref[0]

    def one_head(h, carry):
        acc = jnp.full(bmap.shape, NEG, F32)
        for b in range(N_BUCKETS):
            acc = jnp.where(bmap == b, rel_ref[b, h] * LOG2E, acc)
        o_ref[0, h] = acc
        return carry

    lax.fori_loop(0, A_Q_HEADS, one_head, 0)


def _bias_tables(rel_bias):
    maps = jnp.asarray(_bucket_maps())
    return pl.pallas_call(
        _bias_kernel,
        grid=(3,),
        in_specs=[pl.BlockSpec(memory_space=pltpu.SMEM),
                  pl.BlockSpec((1, BLOCK, 2 * BLOCK), lambda c: (c, 0, 0))],
        out_specs=pl.BlockSpec((1, A_Q_HEADS, BLOCK, 2 * BLOCK), lambda c: (c, 0, 0, 0)),
        out_shape=jax.ShapeDtypeStruct((3, A_Q_HEADS, BLOCK, 2 * BLOCK), F32),
        name="bias_tables",
        compiler_params=_params(1),
    )(rel_bias.astype(F32), maps)


SWA_TILES = A_Q_HEADS // 2
SWA_AHEAD = 4


def _swa_block(sink_ref, zc_ref, zp_ref, zm_ref, bias_ref, o_ref, between=None):
    kv0 = A_WIDTH
    kcat = jnp.concatenate([zm_ref[:, :A_KV_WIDTH], zp_ref[:, :A_KV_WIDTH],
                            zc_ref[:, kv0:kv0 + A_KV_WIDTH]], axis=0)
    vcat = jnp.concatenate([zm_ref[:, A_KV_WIDTH:], zp_ref[:, A_KV_WIDTH:],
                            zc_ref[:, kv0 + A_KV_WIDTH:kv0 + 2 * A_KV_WIDTH]], axis=0)
    gate0 = kv0 + 2 * A_KV_WIDTH
    lane_k = lax.broadcasted_iota(jnp.int32, (3 * BLOCK, BLOCK), 1) < A_HEAD_DIM
    first = lax.broadcasted_iota(jnp.int32, (BLOCK, BLOCK), 1) < A_HEAD_DIM
    own = (lax.broadcasted_iota(jnp.int32, (BLOCK, BLOCK), 0)
           >= lax.broadcasted_iota(jnp.int32, (BLOCK, BLOCK), 1))
    tiles_per_group = A_GROUP // 2
    n_tiles = SWA_TILES
    kv_cache = {}

    def kv_of(g):
        if g not in kv_cache:
            pair = slice((g // 2) * BLOCK, (g // 2 + 1) * BLOCK)
            k2 = kcat[:, pair].astype(F32)
            v2 = vcat[:, pair].astype(F32)
            k2r = pltpu.roll(k2, A_HEAD_DIM, axis=1)
            v2r = pltpu.roll(v2, A_HEAD_DIM, axis=1)
            if g % 2 == 0:
                kd, v_lo, v_hi = jnp.where(lane_k, k2, k2r), v2, v2r
            else:
                kd, v_lo, v_hi = jnp.where(lane_k, k2r, k2), v2r, v2
            vd = [jnp.where(lane_k, v_lo, 1.0).astype(BF16), jnp.where(lane_k, 1.0, v_hi).astype(BF16)]
            kv_cache[g] = (kd.astype(BF16), vd)
        return kv_cache[g]

    def scores(tile):
        qt = zc_ref[:, tile * BLOCK:(tile + 1) * BLOCK]
        zero = jnp.zeros_like(qt)
        qq = jnp.concatenate([jnp.where(first, qt, zero), jnp.where(first, zero, qt)], axis=0)
        kd, _ = kv_of(tile // tiles_per_group)
        return lax.dot_general(qq, kd, (((1,), (1,)), ((), ())), preferred_element_type=F32)

    def finish(tile, res, sink_term):
        cols = slice(tile * BLOCK, (tile + 1) * BLOCK)
        num = jnp.where(first, res[0], res[1])
        den = (pltpu.roll(jnp.where(first, res[1], res[0]), A_HEAD_DIM, axis=1)
               + jnp.where(first, sink_term[0], sink_term[1]))
        gate = zc_ref[:, gate0 + tile * BLOCK:gate0 + (tile + 1) * BLOCK].astype(F32)
        o_ref[:, cols] = (num / den * (gate * jax.nn.sigmoid(gate))).astype(BF16)

    pending = [scores(t) for t in range(SWA_AHEAD)]
    unfinished = None
    for tile in range(n_tiles):
        s = pending.pop(0)
        if tile + SWA_AHEAD < n_tiles:
            pending.append(scores(tile + SWA_AHEAD))
        if between is not None:
            between(tile)
        _, vd = kv_of(tile // tiles_per_group)
        res, sink_term = [], []
        for e in range(2):
            head = 2 * tile + e
            se = s[e * BLOCK:(e + 1) * BLOCK]
            window = jnp.where(own, se[:, 2 * BLOCK:], se[:, BLOCK:2 * BLOCK])
            sc = jnp.concatenate([se[:, :BLOCK], window], axis=1) + bias_ref[0, head]
            sink = sink_ref[head] * LOG2E
            m = jnp.maximum(jnp.max(sc, axis=-1, keepdims=True), sink)
            p = jnp.exp2(sc - m)
            pw = p[:, BLOCK:]
            pf = jnp.concatenate([p[:, :BLOCK], jnp.where(own, 0.0, pw), jnp.where(own, pw, 0.0)],
                                 axis=1).astype(BF16)
            res.append(jnp.dot(pf, vd[e], preferred_element_type=F32))
            sink_term.append(jnp.exp2(sink - m))
        if unfinished is not None:
            finish(*unfinished)
        unfinished = (tile, res, sink_term)
    finish(*unfinished)


OUT_SLABS = 4
NORM_ROWS = BLOCK // (SWA_TILES - 2 * OUT_SLABS)


def _swa_deepnorm(n, attn_refs, w_ref, h_prev, gamma_ref, beta_ref, hf_ref, hb_ref, g_refs, y_ref):
    @pl.when(n == 0)
    def _():
        g_refs[1][...] = jnp.zeros(g_refs[1].shape, BF16)

    def step(g_new, g_prev, attend):
        slab = D_MODEL // OUT_SLABS

        def between(tile):
            if tile % 2 == 0 and tile < 2 * OUT_SLABS:
                cols = slice((tile // 2) * slab, (tile // 2 + 1) * slab)
                y_ref[:, cols] = jnp.dot(g_prev[...], w_ref[:, cols], preferred_element_type=F32)
            elif tile >= 2 * OUT_SLABS:
                r = tile - 2 * OUT_SLABS
                rows = slice(r * NORM_ROWS, (r + 1) * NORM_ROWS)
                t = ALPHA * h_prev(rows) + y_ref[rows]
                mu = jnp.mean(t, axis=-1, keepdims=True)
                d = t - mu
                var = jnp.mean(d * d, axis=-1, keepdims=True)
                out = d * lax.rsqrt(var + LN_EPS) * gamma_ref[...] + beta_ref[...]
                hf_ref[rows] = out
                hb_ref[rows] = out.astype(BF16)

        if attend:
            _swa_block(*attn_refs, g_new, between)
        else:
            g_new[...] = jnp.zeros(g_new.shape, BF16)
            for tile in range(SWA_TILES):
                between(tile)

    idle = (n < META_BLK) | (n == N_BLK)
    for parity in range(2):
        for attend in (True, False):
            pl.when((n % 2 == parity) & (idle != attend))(
                functools.partial(step, g_refs[parity], g_refs[1 - parity], attend))


def _swa_deepnorm_first_kernel(sink_ref, zc_ref, zp_ref, zm_ref, bias_ref, w_ref, head_ref, x_ref,
                               gamma_ref, beta_ref, hf_ref, hb_ref, ga_ref, gb_ref, y_ref):
    n = pl.program_id(0)
    in_front = n - 1 < FRONT // BLOCK
    h_prev = lambda rows: jnp.where(in_front, head_ref[rows], x_ref[rows])
    _swa_deepnorm(n, (sink_ref, zc_ref, zp_ref, zm_ref, bias_ref), w_ref, h_prev, gamma_ref, beta_ref,
                  hf_ref, hb_ref, (ga_ref, gb_ref), y_ref)


def _swa_deepnorm_mid_kernel(sink_ref, zc_ref, zp_ref, zm_ref, bias_ref, w_ref, h_ref,
                             gamma_ref, beta_ref, hf_ref, hb_ref, ga_ref, gb_ref, y_ref):
    _swa_deepnorm(pl.program_id(0), (sink_ref, zc_ref, zp_ref, zm_ref, bias_ref), w_ref,
                  lambda rows: h_ref[rows], gamma_ref, beta_ref, hf_ref, hb_ref, (ga_ref, gb_ref), y_ref)


def _swa_out_ln(z, sinks, bias, w_all, layer, gamma, beta, *, h=None, head=None, x=None):
    d = D_MODEL
    kv_tile = A_WIDTH // (2 * A_KV_WIDTH)
    front_blocks = FRONT // BLOCK
    once = pl.Buffered(1)
    cur = lambda n: jnp.minimum(n, N_BLK - 1)
    prev = lambda n: (jnp.maximum(n - 1, 0), 0)
    fixed = lambda n: (0, 0)
    rows = lambda imap: pl.BlockSpec((BLOCK, d), imap)
    attn = [pl.BlockSpec(memory_space=pltpu.SMEM),
            pl.BlockSpec((BLOCK, A_IN), lambda n: (cur(n), 0)),
            pl.BlockSpec((BLOCK, 2 * A_KV_WIDTH), lambda n: (jnp.maximum(cur(n) - 1, 0), kv_tile)),
            pl.BlockSpec((BLOCK, 2 * A_KV_WIDTH), lambda n: (META_BLK, kv_tile)),
            pl.BlockSpec((1, A_Q_HEADS, BLOCK, 2 * BLOCK),
                         lambda n: (jnp.clip(cur(n) - META_BLK, 0, 2), 0, 0, 0)),
            pl.BlockSpec((None, d, d), lambda n: (layer, 0, 0), pipeline_mode=once)]
    if head is not None:
        kern, args = _swa_deepnorm_first_kernel, (head, x)
        resid = [rows(lambda n: (jnp.clip(n - 1, 0, front_blocks - 1), 0)),
                 rows(lambda n: (jnp.maximum(n - 1 - front_blocks, 0), 0))]
    else:
        kern, args, resid = _swa_deepnorm_mid_kernel, (h,), [rows(prev)]
    return pl.pallas_call(
        kern,
        grid=(N_BLK + 1,),
        in_specs=attn + resid + [pl.BlockSpec((1, d), fixed), pl.BlockSpec((1, d), fixed)],
        out_specs=[rows(prev), rows(prev)],
        out_shape=[jax.ShapeDtypeStruct((L_PAD, d), F32), jax.ShapeDtypeStruct((L_PAD, d), BF16)],
        scratch_shapes=[pltpu.VMEM((BLOCK, A_WIDTH), BF16), pltpu.VMEM((BLOCK, A_WIDTH), BF16),
                        pltpu.VMEM((BLOCK, d), F32)],
        name="swa_mixer_out_proj_ln",
        compiler_params=_params(1),
    )(sinks.astype(F32), z, z, z, bias, w_all, *args, gamma, beta)


SOFTPLUS2_DIRECT_MAX = 126.0


def _softplus2(s):
    return jnp.maximum(s, jnp.log2(1.0 + jnp.exp2(jnp.minimum(s, SOFTPLUS2_DIRECT_MAX))))


def _sb_kernel(q_ref, k_ref, v_ref, gate_ref, o_ref, acc_ref, later_ref, dead_ref):
    qi = pl.program_id(1)

    def iotas(width):
        return (lax.broadcasted_iota(jnp.int32, (SB_CH, width), 0),
                lax.broadcasted_iota(jnp.int32, (SB_CH, width), 1))

    suffix = {w: (lax.broadcasted_iota(jnp.int32, (w, w), 0)
                  >= lax.broadcasted_iota(jnp.int32, (w, w), 1)).astype(BF16) for w in (SB_CH, SB_TK)}
    rr, cc = iotas(SB_CH)
    _, cc_wide = iotas(SB_TK)
    pad_col = lambda start: META_ROW0 - start
    n_chunks = SB_TQ // SB_CH

    def visible(kind, start, width):
        if kind is None:
            return None
        if kind == "tri":
            return cc < rr
        if kind == "tri_pad":
            return (cc < rr) & (cc >= pad_col(start))
        assert kind == "pad" and width == SB_TK
        return cc_wide >= pad_col(start)

    def write_gated(rows, lanes, acc):
        gate = gate_ref[rows, lanes].astype(F32)
        o_ref[rows, lanes] = (acc * (gate * jax.nn.sigmoid(gate))).astype(BF16)

    def sweep(spans, first=False):
        items = [(start, width, visible(kind, start, width), c, hh)
                 for (start, width, kind, c) in spans for hh in range(SB_HG)]
        rows = lambda c: slice(c * SB_CH, (c + 1) * SB_CH)
        lanes = lambda hh: slice(hh * B_HEAD_DIM, (hh + 1) * B_HEAD_DIM)
        keys = lambda start, width: pl.ds(start if isinstance(start, int) else pl.multiple_of(start, SB_CH), width)
        later = {(c, hh): (jnp.zeros((SB_CH, B_HEAD_DIM), F32) if first else later_ref[hh, rows(c)])
                 for (_, _, _, c) in spans for hh in range(SB_HG)}
        acc, s, tw, later_before = {}, {}, {}, {}

        def stage_scores(n):
            start, width, _, c, hh = items[n]
            k = k_ref[keys(start, width), lanes(hh)]
            s[n] = lax.dot_general(q_ref[rows(c), lanes(hh)], k, (((1,), (1,)), ((), ())),
                                   preferred_element_type=F32)

        def stage_suffix(n):
            _, width, vis, c, hh = items[n]
            sp = _softplus2(s[n])
            if vis is not None:
                sp = jnp.where(vis, sp, 0.0)
            tw[n] = jnp.dot(sp.astype(BF16), suffix[width], preferred_element_type=F32)
            tot = jnp.sum(sp, axis=-1, keepdims=True)
            later_before[n] = later[(c, hh)]
            later[(c, hh)] = later_before[n] + jnp.broadcast_to(tot, later_before[n].shape)

        def stage_values(n):
            start, width, vis, c, hh = items[n]
            lat = jnp.concatenate([later_before.pop(n)] * (width // B_HEAD_DIM), axis=1)
            p = jnp.exp2(s.pop(n) - tw.pop(n) - lat)
            if vis is not None:
                p = jnp.where(vis, p, 0.0)
            v = v_ref[keys(start, width), lanes(hh)]
            pv = jnp.dot(p.astype(BF16), v, preferred_element_type=F32)
            acc[(c, hh)] = acc[(c, hh)] + pv if (c, hh) in acc else pv

        skew = 2
        for step in range(len(items) + 2 * skew):
            if step < len(items):
                stage_scores(step)
            if 0 <= step - skew < len(items):
                stage_suffix(step - skew)
                if step - skew == len(items) - 1:
                    least = functools.reduce(jnp.minimum, later.values())
                    dead_ref[0] = (jnp.min(least) >= SB_DEAD).astype(jnp.int32)
            if 0 <= step - 2 * skew < len(items):
                stage_values(step - 2 * skew)
        for (c, hh), a in acc.items():
            if first:
                acc_ref[hh, rows(c)] = a
                write_gated(rows(c), lanes(hh), a)
            else:
                acc_ref[hh, rows(c)] += a
            later_ref[hh, rows(c)] = later[(c, hh)]

    assert SB_TQ % SB_TK == 0 and FRONT % SB_TK == 0
    chunks = range(n_chunks)
    own = lambda c: qi * SB_TQ + c * SB_CH

    def to_boundary(start, c):
        full = [(st, SB_TK, None, c) for st in range(start, FRONT - 1, -SB_TK)]
        last = full[-1][0] - SB_TK if full else start
        assert last <= META_ROW0 < last + SB_TK
        return full + [(last, SB_TK, "pad", c)]

    @pl.when(qi == 0)
    def _():
        meta_chunk = META_ROW0 // SB_CH
        acc_ref[:, :meta_chunk * SB_CH] = jnp.zeros((SB_HG, meta_chunk * SB_CH, B_HEAD_DIM), F32)
        o_ref[:meta_chunk * SB_CH] = jnp.zeros((meta_chunk * SB_CH, SB_HG * B_HEAD_DIM), BF16)
        spans = [(meta_chunk * SB_CH, SB_CH, "tri_pad", meta_chunk)]
        for c in chunks[meta_chunk + 1:]:
            spans += [(c * SB_CH, SB_CH, "tri", c)] + to_boundary(c * SB_CH - SB_TK, c)
        sweep(spans, first=True)

    @pl.when(qi >= 1)
    def _():
        sweep([(own(c), SB_CH, "tri", c) for c in chunks]
              + [(own(c) - SB_TK, SB_TK, None, c) for c in chunks], first=True)

    sweep_goes_on = (qi >= 1) & (dead_ref[0] == 0)

    def body(t):
        sweep([(own(c) - t * SB_TK, SB_TK, None, c) for c in chunks])
        return t + 1

    lax.while_loop(lambda t: (own(0) - t * SB_TK >= FRONT) & (dead_ref[0] == 0), body, jnp.int32(2))

    @pl.when((qi >= 1) & (dead_ref[0] == 0))
    def _():
        sweep([span for c in chunks for span in to_boundary(FRONT - SB_TK + c * SB_CH, c)])

    @pl.when(sweep_goes_on)
    def _():
        for hh in range(SB_HG):
            lanes = slice(hh * B_HEAD_DIM, (hh + 1) * B_HEAD_DIM)
            write_gated(slice(None), lanes, acc_ref[hh])


def _stick_breaking(z):
    width = SB_HG * B_HEAD_DIM
    hcols = B_WIDTH // width
    return pl.pallas_call(
        _sb_kernel,
        grid=(B_HEADS // SB_HG, L_PAD // SB_TQ),
        in_specs=[pl.BlockSpec((SB_TQ, width), lambda h, i: (i, h)),
                  pl.BlockSpec((L_PAD, width), lambda h, i: (0, hcols + h)),
                  pl.BlockSpec((L_PAD, width), lambda h, i: (0, 2 * hcols + h)),
                  pl.BlockSpec((SB_TQ, width), lambda h, i: (i, 3 * hcols + h))],
        out_specs=pl.BlockSpec((SB_TQ, width), lambda h, i: (i, h)),
        out_shape=jax.ShapeDtypeStruct((L_PAD, B_WIDTH), BF16),
        scratch_shapes=[pltpu.VMEM((SB_HG, SB_TQ, B_HEAD_DIM), F32),
                        pltpu.VMEM((SB_HG, SB_TQ, B_HEAD_DIM), F32),
                        pltpu.SMEM((1,), jnp.int32)],
        name="stick_breaking_mixer",
        compiler_params=_params(2),
    )(z, z, z, z)


def _q_scale(n_cols, q_cols, scale):
    s = np.ones((1, n_cols), np.float32)
    s[:, :q_cols] = scale
    return jnp.asarray(s)


def _forward_one(x, meta_tokens, rel_bias, w_in_a, sinks_a, w_out_a, w_in_b, w_out_b, ln_g, ln_b):
    x = x.astype(F32)
    head = jnp.concatenate([jnp.zeros((META_ROW0, D_MODEL), F32), meta_tokens.astype(F32)], axis=0)
    bias = _bias_tables(rel_bias)
    scale_a = _q_scale(A_IN, A_WIDTH, A_HEAD_DIM ** -0.5 * LOG2E)
    scale_b = _q_scale(B_IN, B_WIDTH, B_HEAD_DIM ** -0.5 * LOG2E)
    w_in = (w_in_a.astype(F32), w_in_b.astype(F32))
    w_out = (w_out_a.astype(BF16), w_out_b.astype(BF16))
    in_scale = (scale_a, scale_b)
    in_tn = (1536, 1024)
    h = hb = None
    for i in range(DEPTH):
        j, mixer = i // 2, i % 2
        if i == 0:
            z = _in_proj(x, w_in[mixer], j, in_scale[mixer], FRONT, in_tn[mixer], head=head)
        else:
            z = _in_proj(hb, w_in[mixer], j, in_scale[mixer], 1536, in_tn[mixer])
        gamma, beta = ln_g[i][None, :].astype(F32), ln_b[i][None, :].astype(F32)
        if mixer == 0:
            resid = dict(head=head, x=x) if i == 0 else dict(h=h)
            h, hb = _swa_out_ln(z, sinks_a[j], bias, w_out[mixer], j, gamma, beta, **resid)
        elif i == DEPTH - 1:
            return _out_ln(_stick_breaking(z), w_out[mixer], j, gamma, beta, h, last=True)
        else:
            h, hb = _out_ln(_stick_breaking(z), w_out[mixer], j, gamma, beta, h)


def kernel(x, meta_tokens, rel_bias, w_in_a, sinks_a, w_out_a, w_in_b, w_out_b, ln_g, ln_b):
    batch = x.shape[0]
    assert x.shape[1:] == (SEQ, D_MODEL)
    xs = x.reshape(batch * SEQ, D_MODEL)
    outs = [_forward_one(xs[b * SEQ:(b + 1) * SEQ] if batch > 1 else xs, meta_tokens, rel_bias,
                         w_in_a, sinks_a, w_out_a, w_in_b, w_out_b, ln_g, ln_b) for b in range(batch)]
    out = outs[0] if batch == 1 else jnp.concatenate(outs, axis=0)
    return out.reshape(batch, SEQ, D_MODEL).astype(x.dtype)
```
